```python
import math
import jax, jax.numpy as jnp
from jax import lax
import numpy as np

D_MODEL = 2048
BATCH = 2
SEQ = 8192
DEPTH = 2

GRID_W = 64
CTX_LEN = 256
EPS = 1e-6
MASK_VALUE = -1e30

HEAD_DIM = 128
NA_WIDTH = D_MODEL // 4
NA_HEADS = NA_WIDTH // HEAD_DIM
NA_WIN_R = 8
NA_WIN_C = 16
WA_WIDTH = D_MODEL // 4
WA_HEADS = WA_WIDTH // HEAD_DIM
WA_KV_HEADS = 2
WA_KV_WIDTH = WA_KV_HEADS * HEAD_DIM
WA_WINDOW = 128
WA_BLOCK = 128
ROPE_BASE = 10000.0
SSD_D_INNER = D_MODEL // 2
SSD_HEAD_DIM = 64
SSD_HEADS = SSD_D_INNER // SSD_HEAD_DIM
SSD_GROUPS = 2
SSD_D_STATE = 128
SSD_CONV = 5
SSD_CHUNK = 128
SSD_XBC_WIDTH = SSD_D_INNER + 2 * SSD_GROUPS * SSD_D_STATE
MIX_WIDTH = NA_WIDTH + WA_WIDTH + SSD_D_INNER
IN_SIZES = (NA_WIDTH, NA_WIDTH, NA_WIDTH, WA_WIDTH, WA_KV_WIDTH, WA_KV_WIDTH, SSD_D_INNER, SSD_XBC_WIDTH, 2 * SSD_HEADS)
IN_WIDTH = 3 * NA_WIDTH + WA_WIDTH + 2 * WA_KV_WIDTH + SSD_D_INNER + SSD_XBC_WIDTH + 2 * SSD_HEADS
N_EXPERTS = 32
N_EXPERT_GROUPS = 4
EXPERTS_PER_GROUP = N_EXPERTS // N_EXPERT_GROUPS
TOP_K = 2
D_FF_EXPERT = D_MODEL // 2
MOE_BLOCK = 128

kernel_name = 'hybrid_na_swa_ssd_moe_dit'

f32 = jnp.float32


def rms_norm(x, gain):
    xf = x.astype(f32)
    y = xf * lax.rsqrt(jnp.mean(xf * xf, axis=-1, keepdims=True) + EPS)
    return (y * gain.astype(f32)).astype(x.dtype)


def split_columns(u):
    parts, off = [], 0
    for size in IN_SIZES:
        parts.append(u[..., off:off + size])
        off += size
    return parts


def heads(t, n):
    return t.reshape(t.shape[0], t.shape[1], n, HEAD_DIM)


def axial_rope_tables(n_tokens):
    t = jnp.arange(n_tokens, dtype=jnp.int32)
    m = HEAD_DIM // 4
    inv = jnp.power(ROPE_BASE, -jnp.arange(m, dtype=f32) / m)
    ang_r = (t // GRID_W).astype(f32)[:, None] * inv
    ang_c = (t % GRID_W).astype(f32)[:, None] * inv
    return (jnp.cos(ang_r), jnp.sin(ang_r), jnp.cos(ang_c), jnp.sin(ang_c))


def rope_1d(x, cos, sin):
    x1, x2 = jnp.split(x, 2, axis=-1)
    cos = cos[:, None, :].astype(x.dtype)
    sin = sin[:, None, :].astype(x.dtype)
    return jnp.concatenate([x1 * cos - x2 * sin, x2 * cos + x1 * sin], axis=-1)


def axial_rope(x, rope):
    cos_r, sin_r, cos_c, sin_c = rope
    xr, xc = jnp.split(x, 2, axis=-1)
    return jnp.concatenate([rope_1d(xr, cos_r, sin_r), rope_1d(xc, cos_c, sin_c)], axis=-1)


def context_attention(q, k, v, sink):
    b, l = q.shape[:2]
    s = jnp.einsum('blhgd,bmhd->bhglm', q, k).astype(f32) * (q.shape[-1] ** -0.5)
    if sink is not None:
        s = jnp.concatenate([s, jnp.broadcast_to(sink.astype(f32)[None, :, :, None, None], s.shape[:-1] + (1,))], axis=-1)
    p = jax.nn.softmax(s, axis=-1)
    if sink is not None:
        p = p[..., :-1]
    o = jnp.einsum('bhglm,bmhd->blhgd', p.astype(v.dtype), v)
    return o.reshape(b, l, -1)


def neighbourhood_attention(q, k, v, kc, vc, rpb):
    b, s, h, dh = q.shape
    rows = s // GRID_W
    kr = min(NA_WIN_R, rows)
    r = jnp.arange(rows)
    key_rows = jnp.clip(r - kr // 2, 0, rows - kr)[:, None] + jnp.arange(kr)[None, :]
    grid = lambda t: t.reshape(b, rows, GRID_W, h, dh)
    nk = kr * GRID_W
    kg = jnp.take(grid(k), key_rows, axis=1).reshape(b, rows, nk, h, dh)
    vg = jnp.take(grid(v), key_rows, axis=1).reshape(b, rows, nk, h, dh)
    col = jnp.arange(GRID_W)
    c0 = jnp.clip(col - NA_WIN_C // 2, 0, GRID_W - NA_WIN_C)
    col_ok = (col[None, :] >= c0[:, None]) & (col[None, :] < c0[:, None] + NA_WIN_C)
    dr = key_rows - r[:, None] + NA_WIN_R - 1
    dc = jnp.clip(col[None, :] - col[:, None] + NA_WIN_C - 1, 0, 2 * NA_WIN_C - 2)
    bias = rpb.astype(f32)[:, dr[:, None, :, None], dc[None, :, None, :]]
    bias = jnp.where(col_ok[None, None, :, None, :], bias, MASK_VALUE).reshape(h, rows, GRID_W, nk)
    scale = dh ** -0.5
    qg = grid(q)
    s_loc = jnp.einsum('brqhd,brkhd->bhrqk', qg, kg).astype(f32) * scale + bias
    s_ctx = jnp.einsum('brqhd,bchd->bhrqc', qg, kc).astype(f32) * scale
    p = jax.nn.softmax(jnp.concatenate([s_loc, s_ctx], axis=-1), axis=-1).astype(v.dtype)
    o = jnp.einsum('bhrqk,brkhd->brqhd', p[..., :nk], vg) + jnp.einsum('bhrqc,bchd->brqhd', p[..., nk:], vc)
    return o.reshape(b, s, h * dh)


def window_attention(q, k, v, kc, vc, sink):
    b, s, hq, dh = q.shape
    hkv = k.shape[2]
    g = hq // hkv
    nb = s // WA_BLOCK
    nk = WA_BLOCK + 2 * WA_WINDOW
    band = jnp.arange(nb)[:, None] * WA_BLOCK + jnp.arange(nk)[None, :]
    pad = ((0, 0), (WA_WINDOW, WA_WINDOW), (0, 0), (0, 0))
    kb = jnp.take(jnp.pad(k, pad), band, axis=1)
    vb = jnp.take(jnp.pad(v, pad), band, axis=1)
    qpos = jnp.arange(s).reshape(nb, WA_BLOCK)
    kpos = band - WA_WINDOW
    ok = (jnp.abs(qpos[:, :, None] - kpos[:, None, :]) <= WA_WINDOW) & (kpos[:, None, :] >= 0) & (kpos[:, None, :] < s)
    qb = q.reshape(b, nb, WA_BLOCK, hkv, g, dh)
    scale = dh ** -0.5
    s_loc = jnp.where(ok, jnp.einsum('bnqhgd,bnkhd->bhgnqk', qb, kb).astype(f32) * scale, MASK_VALUE)
    s_ctx = jnp.einsum('bnqhgd,bchd->bhgnqc', qb, kc).astype(f32) * scale
    sink_col = jnp.broadcast_to(sink.astype(f32).reshape(1, hkv, g, 1, 1, 1), s_loc.shape[:-1] + (1,))
    p = jax.nn.softmax(jnp.concatenate([s_loc, s_ctx, sink_col], axis=-1), axis=-1).astype(v.dtype)
    cl = kc.shape[1]
    o = jnp.einsum('bhgnqk,bnkhd->bnqhgd', p[..., :nk], vb) + jnp.einsum('bhgnqc,bchd->bnqhgd', p[..., nk:nk + cl], vc)
    return o.reshape(b, s, hq * dh)


def centred_dwconv(x, w, bias):
    kw, ch = w.shape
    pad = kw // 2
    y = lax.conv_general_dilated(x, w[:, None, :].astype(x.dtype), window_strides=(1,), padding=[(pad, kw - 1 - pad)],
                                 dimension_numbers=('NWC', 'WIO', 'NWC'), feature_group_count=ch)
    return y + bias.astype(x.dtype)


def ssd_inputs(xbc, dt_raw, conv_w, conv_b, dt_bias):
    xbc = jax.nn.silu(centred_dwconv(xbc, conv_w, conv_b)).astype(f32)
    b, l, _ = xbc.shape
    xs, bm, cm = jnp.split(xbc, [SSD_D_INNER, SSD_D_INNER + SSD_GROUPS * SSD_D_STATE], axis=-1)
    xs = xs.reshape(b, l, SSD_HEADS, SSD_HEAD_DIM)
    bm = bm.reshape(b, l, SSD_GROUPS, SSD_D_STATE)
    cm = cm.reshape(b, l, SSD_GROUPS, SSD_D_STATE)
    dt = jax.nn.softplus(dt_raw.astype(f32).reshape(b, l, 2, SSD_HEADS) + dt_bias.astype(f32))
    return (xs, bm, cm, dt)


def _ssd_chunks(x, dt, a, bm):
    b, l, h, p = x.shape
    g, n = bm.shape[2], bm.shape[3]
    k = h // g
    nc = l // SSD_CHUNK
    xc = x.reshape(b, nc, SSD_CHUNK, g, k, p)
    dtc = dt.reshape(b, nc, SSD_CHUNK, g, k)
    bc = bm.reshape(b, nc, SSD_CHUNK, g, n)
    acs = jnp.cumsum(dtc * a.reshape(g, k), axis=2)
    states = jnp.einsum('bclgn,bclgkp->bcgkpn', bc, xc * (jnp.exp(acs[:, :, -1:] - acs) * dtc)[..., None])
    return xc, dtc, acs, states.reshape(b, nc, h, p, n), jnp.exp(acs[:, :, -1]).reshape(b, nc, h)


def _ssd_state_scan(states, chunk_decay, h0):
    def step(hc, inp):
        st, dec = inp
        return dec[..., None, None] * hc + st, hc
    h_final, h_start = lax.scan(step, h0, (jnp.swapaxes(states, 0, 1), jnp.swapaxes(chunk_decay, 0, 1)))
    return jnp.swapaxes(h_start, 0, 1), h_final


def ssd_final_state(x, dt, a, bm, h0):
    _, _, _, states, decay = _ssd_chunks(x, dt, a, bm)
    return _ssd_state_scan(states, decay, h0)[1]


def ssd_scan(x, dt, a, bm, cm, d_skip, h0):
    b, l, h, p = x.shape
    g, n = bm.shape[2], bm.shape[3]
    k = h // g
    nc = l // SSD_CHUNK
    xc, dtc, acs, states, decay = _ssd_chunks(x, dt, a, bm)
    h_start, h_final = _ssd_state_scan(states, decay, h0)
    bc = bm.reshape(b, nc, SSD_CHUNK, g, n)
    cc = cm.reshape(b, nc, SSD_CHUNK, g, n)
    cb = jnp.einsum('bclgn,bcsgn->bcgls', cc, bc)
    acs_t = jnp.moveaxis(acs, 2, -1)
    dt_t = jnp.moveaxis(dtc, 2, -1)
    causal_in_chunk = jnp.tril(jnp.ones((SSD_CHUNK, SSD_CHUNK), dtype=bool))
    seg = jnp.exp(jnp.where(causal_in_chunk, acs_t[..., :, None] - acs_t[..., None, :], -jnp.inf))
    y = jnp.einsum('bcgkls,bcsgkp->bclgkp', cb[:, :, :, None] * seg * dt_t[..., None, :], xc)
    y = y + jnp.einsum('bclgn,bcgkpn->bclgkp', cc, h_start.reshape(b, nc, g, k, p, n)) * jnp.exp(acs)[..., None]
    return y.reshape(b, l, h, p) + x * d_skip[:, None], h_final


def bidirectional_ssd(lat, ctx, a_log, d_skip, ctx_out):
    xl, bl, cl, dtl = lat
    xc, bc, cc, dtc = ctx
    a = -jnp.exp(a_log.astype(f32))
    d = d_skip.astype(f32)
    h0 = jnp.zeros((xl.shape[0], SSD_HEADS, SSD_HEAD_DIM, SSD_D_STATE), f32)

    def run(direction):
        f = (lambda t: jnp.flip(t, axis=1)) if direction == 1 else (lambda t: t)
        if ctx_out:
            yc, hc = ssd_scan(f(xc), f(dtc[:, :, direction]), a[direction], f(bc), f(cc), d[direction], h0)
            yc = f(yc)
        else:
            yc, hc = None, ssd_final_state(f(xc), f(dtc[:, :, direction]), a[direction], f(bc), h0)
        yl, _ = ssd_scan(f(xl), f(dtl[:, :, direction]), a[direction], f(bl), f(cl), d[direction], hc)
        return f(yl), yc

    yl_f, yc_f = run(0)
    yl_b, yc_b = run(1)
    return yl_f + yl_b, (yc_f + yc_b if ctx_out else None)


def gated_group_norm(y, z, gain):
    b, l = z.shape[:2]
    u = y.reshape(b, l, SSD_GROUPS, -1) * jax.nn.silu(z.astype(f32)).reshape(b, l, SSD_GROUPS, -1)
    u = u * lax.rsqrt(jnp.mean(u * u, axis=-1, keepdims=True) + EPS)
    return (u.reshape(b, l, -1) * gain.astype(f32)).astype(z.dtype)


def token_mixers(hl, hc, w_in, w_out, rpb, sink, conv_w, conv_b, a_log, dt_bias, d_skip, ssd_norm, rope, ctx_out):
    b, cl = hc.shape[0], hc.shape[1]
    ql_a, kl_a, vl_a, ql_b, kl_b, vl_b, zl, xbcl, dtl = split_columns(hl @ w_in)
    qc_a, kc_a, vc_a, qc_b, kc_b, vc_b, zc, xbcc, dtc = split_columns(hc @ w_in)
    kc_a_h, vc_a_h = heads(kc_a, NA_HEADS), heads(vc_a, NA_HEADS)
    o_a = neighbourhood_attention(heads(ql_a, NA_HEADS), heads(kl_a, NA_HEADS), heads(vl_a, NA_HEADS), kc_a_h, vc_a_h, rpb)
    kc_b_h, vc_b_h = heads(kc_b, WA_KV_HEADS), heads(vc_b, WA_KV_HEADS)
    o_b = window_attention(axial_rope(heads(ql_b, WA_HEADS), rope), axial_rope(heads(kl_b, WA_KV_HEADS), rope),
                           heads(vl_b, WA_KV_HEADS), kc_b_h, vc_b_h, sink)
    yl, yc = bidirectional_ssd(ssd_inputs(xbcl, dtl, conv_w, conv_b, dt_bias), ssd_inputs(xbcc, dtc, conv_w, conv_b, dt_bias),
                               a_log, d_skip, ctx_out)
    out_l = jnp.concatenate([o_a, o_b, gated_group_norm(yl, zl, ssd_norm)], axis=-1) @ w_out
    if not ctx_out:
        return out_l, None
    g_wa = WA_HEADS // WA_KV_HEADS
    oc_a = context_attention(heads(qc_a, NA_HEADS)[:, :, :, None], kc_a_h, vc_a_h, None)
    oc_b = context_attention(heads(qc_b, WA_HEADS).reshape(b, cl, WA_KV_HEADS, g_wa, HEAD_DIM), kc_b_h, vc_b_h,
                             sink.reshape(WA_KV_HEADS, g_wa))
    out_c = jnp.concatenate([oc_a, oc_b, gated_group_norm(yc, zc, ssd_norm)], axis=-1) @ w_out
    return out_l, out_c


def moe_ffn(h, w_router, router_bias, w_gate, w_up, w_down):
    t, d = h.shape
    aff = jax.nn.sigmoid((h @ w_router).astype(f32))
    sel = (aff + router_bias.astype(f32)).reshape(t, N_EXPERT_GROUPS, EXPERTS_PER_GROUP)
    grp = jnp.argmax(lax.top_k(sel, 2)[0].sum(-1), axis=-1)
    local = lax.top_k(jnp.take_along_axis(sel, grp[:, None, None], axis=1)[:, 0], TOP_K)[1]
    expert = grp[:, None] * EXPERTS_PER_GROUP + local
    gate = jnp.take_along_axis(aff, expert, axis=1)
    gate = gate / jnp.sum(gate, axis=-1, keepdims=True)
    n = t * TOP_K
    flat = expert.reshape(n)
    order = jnp.argsort(flat)
    e_sorted = flat[order]
    tok = order // TOP_K
    counts = jnp.bincount(flat, length=N_EXPERTS)
    padded = (counts + MOE_BLOCK - 1) // MOE_BLOCK * MOE_BLOCK
    pend = jnp.cumsum(padded)
    slot = (pend - padded)[e_sorted] + jnp.arange(n) - (jnp.cumsum(counts) - counts)[e_sorted]
    n_blocks = -(-n // MOE_BLOCK) + N_EXPERTS
    block_expert = jnp.minimum(jnp.searchsorted(pend, jnp.arange(n_blocks) * MOE_BLOCK, side='right'), N_EXPERTS - 1)
    xbuf = jnp.zeros((n_blocks * MOE_BLOCK, d), h.dtype).at[slot].set(h[tok])

    def expert_block(args):
        xb, e = args
        return (jax.nn.silu(xb @ w_gate[e]) * (xb @ w_up[e])) @ w_down[e]

    ybuf = lax.map(expert_block, (xbuf.reshape(n_blocks, MOE_BLOCK, d), block_expert))
    y = ybuf.reshape(-1, d)[slot] * gate.reshape(n)[order][:, None].astype(h.dtype)
    return jnp.zeros_like(h).at[tok].add(y)


def setup_inputs(seed: int = 0) -> dict:
    key = jax.random.key(seed)
    ks = jax.random.split(key, 24)
    nrm = lambda k, shape, scale: jax.random.normal(k, shape, f32) * scale
    D = D_MODEL
    dt0 = jnp.exp(jax.random.uniform(ks[16], (DEPTH, 2, SSD_HEADS), f32, math.log(1e-3), math.log(1e-1)))
    return {
        'x': nrm(ks[0], (BATCH, SEQ, D), 1.0),
        'c': nrm(ks[1], (BATCH, D), 1.0),
        'ctx': nrm(ks[2], (BATCH, CTX_LEN, D), 1.0),
        'c_ctx': nrm(ks[3], (D,), 1.0),
        'w_ada': nrm(ks[4], (DEPTH, D, 6 * D), 0.5 * D ** -0.5),
        'b_ada': nrm(ks[5], (DEPTH, 6 * D), 0.02),
        'norm_mix': 1.0 + nrm(ks[6], (DEPTH, D), 0.02),
        'norm_ffn': 1.0 + nrm(ks[7], (DEPTH, D), 0.02),
        'norm_final': 1.0 + nrm(ks[8], (D,), 0.02),
        'w_in': nrm(ks[9], (DEPTH, D, IN_WIDTH), D ** -0.5),
        'w_out': nrm(ks[10], (DEPTH, MIX_WIDTH, D), MIX_WIDTH ** -0.5),
        'na_rpb': nrm(ks[11], (DEPTH, NA_HEADS, 2 * NA_WIN_R - 1, 2 * NA_WIN_C - 1), 0.1),
        'wa_sink': nrm(ks[12], (DEPTH, WA_HEADS), 0.5),
        'ssd_conv_w': nrm(ks[13], (DEPTH, SSD_CONV, SSD_XBC_WIDTH), SSD_CONV ** -0.5),
        'ssd_conv_b': nrm(ks[14], (DEPTH, SSD_XBC_WIDTH), 0.02),
        'ssd_a_log': jnp.log(jax.random.uniform(ks[15], (DEPTH, 2, SSD_HEADS), f32, 1.0, 16.0)),
        'ssd_dt_bias': dt0 + jnp.log(-jnp.expm1(-dt0)),
        'ssd_d': 1.0 + nrm(ks[17], (DEPTH, 2, SSD_HEADS), 0.1),
        'ssd_norm': 1.0 + nrm(ks[18], (DEPTH, SSD_D_INNER), 0.02),
        'w_router': nrm(ks[19], (D, N_EXPERTS), D ** -0.5),
        'router_bias': nrm(ks[20], (N_EXPERTS,), 0.01),
        'w_gate': nrm(ks[21], (DEPTH, N_EXPERTS, D, D_FF_EXPERT), D ** -0.5),
        'w_up': nrm(ks[22], (DEPTH, N_EXPERTS, D, D_FF_EXPERT), D ** -0.5),
        'w_down': nrm(ks[23], (DEPTH, N_EXPERTS, D_FF_EXPERT, D), D_FF_EXPERT ** -0.5),
    }


def reference(x, c, ctx, c_ctx, w_ada, b_ada, norm_mix, norm_ffn, norm_final, w_in, w_out, na_rpb, wa_sink,
              ssd_conv_w, ssd_conv_b, ssd_a_log, ssd_dt_bias, ssd_d, ssd_norm, w_router, router_bias,
              w_gate, w_up, w_down):
    b, s, d = x.shape
    rope = axial_rope_tables(s)
    silu_c = jax.nn.silu(c)
    silu_cc = jax.nn.silu(c_ctx)
    xc = ctx
    for l in range(DEPTH):
        ctx_out = l < DEPTH - 1
        mod = silu_c @ w_ada[l] + b_ada[l]
        sh_m, sc_m, g_m, sh_f, sc_f, g_f = jnp.split(mod[:, None, :], 6, axis=-1)
        cmod = silu_cc @ w_ada[l] + b_ada[l]
        csh_m, csc_m, cg_m, csh_f, csc_f, cg_f = jnp.split(cmod, 6)
        hl = rms_norm(x, norm_mix[l]) * (1.0 + sc_m) + sh_m
        hc = rms_norm(xc, norm_mix[l]) * (1.0 + csc_m) + csh_m
        o_l, o_c = token_mixers(hl, hc, w_in[l], w_out[l], na_rpb[l], wa_sink[l], ssd_conv_w[l], ssd_conv_b[l],
                                ssd_a_log[l], ssd_dt_bias[l], ssd_d[l], ssd_norm[l], rope, ctx_out)
        x = x + g_m * o_l
        hf = rms_norm(x, norm_ffn[l]) * (1.0 + sc_f) + sh_f
        if ctx_out:
            xc = xc + cg_m * o_c
            hfc = rms_norm(xc, norm_ffn[l]) * (1.0 + csc_f) + csh_f
            y = moe_ffn(jnp.concatenate([hf.reshape(-1, d), hfc.reshape(-1, d)], axis=0),
                        w_router, router_bias, w_gate[l], w_up[l], w_down[l])
            x = x + g_f * y[:b * s].reshape(b, s, d)
            xc = xc + cg_f * y[b * s:].reshape(xc.shape)
        else:
            x = x + g_f * moe_ffn(hf.reshape(-1, d), w_router, router_bias, w_gate[l], w_up[l], w_down[l]).reshape(b, s, d)
    return rms_norm(x, norm_final)
```

```python
import functools

import jax
import jax.numpy as jnp
from jax import lax
from jax.experimental import pallas as pl
from jax.experimental.pallas import tpu as pltpu

f32 = jnp.float32
bf16 = jnp.bfloat16
i32 = jnp.int32

D_MODEL = 2048
GRID_W = 64
CTX_LEN = 256
EPS = 1e-6
MASK_VALUE = -1e30
HEAD_DIM = 128
NA_WIDTH = D_MODEL // 4
NA_HEADS = NA_WIDTH // HEAD_DIM
NA_WIN_R = 8
NA_WIN_C = 16
WA_WIDTH = D_MODEL // 4
WA_HEADS = WA_WIDTH // HEAD_DIM
WA_KV_HEADS = 2
WA_KV_WIDTH = WA_KV_HEADS * HEAD_DIM
WA_WINDOW = 128
WA_BLOCK = 128
ROPE_BASE = 10000.0
SSD_D_INNER = D_MODEL // 2
SSD_HEAD_DIM = 64
SSD_HEADS = SSD_D_INNER // SSD_HEAD_DIM
SSD_GROUPS = 2
SSD_D_STATE = 128
SSD_CONV = 5
SSD_BC_WIDTH = 2 * SSD_GROUPS * SSD_D_STATE
SSD_XBC_WIDTH = SSD_D_INNER + SSD_BC_WIDTH
SSD_GROUP_WIDTH = SSD_D_INNER // SSD_GROUPS
SSD_HEADS_PER_GROUP = SSD_HEADS // SSD_GROUPS
N_EXPERTS = 32
N_EXPERT_GROUPS = 4
EXPERTS_PER_GROUP = N_EXPERTS // N_EXPERT_GROUPS
D_FF_EXPERT = D_MODEL // 2

QKV_WIDTH = 3 * NA_WIDTH + WA_WIDTH + 2 * WA_KV_WIDTH
ROPE_LO = 3 * NA_WIDTH
ROPE_HI = ROPE_LO + WA_WIDTH + WA_KV_WIDTH
Z_LO = QKV_WIDTH
XBC_LO = Z_LO + SSD_D_INNER
DT_LO = XBC_LO + SSD_XBC_WIDTH
DT_WIDTH = 2 * SSD_HEADS

ROW_BLOCK = 256
SSD_CHUNK = 128
NA_GROUP_ROWS = 4
NA_KEY_ROWS = NA_GROUP_ROWS + NA_WIN_R
MOE_BLOCK = 256
CONV_HALO = 8
VMEM_LIMIT = 56 * 1024 * 1024


def _cparams(sem):
    return pltpu.CompilerParams(dimension_semantics=sem, vmem_limit_bytes=VMEM_LIMIT)


def _sigmoid(x):
    return 1.0 / (1.0 + jnp.exp(-x))


def _silu(x):
    return x * _sigmoid(x)


def _softplus(x):
    return jnp.maximum(x, 0.0) + jnp.log1p(jnp.exp(-jnp.abs(x)))


def _dot(a, b):
    return jnp.dot(a, b, preferred_element_type=f32)


def _dot_nt(a, b):
    return lax.dot_general(a, b, (((1,), (1,)), ((), ())), preferred_element_type=f32)


def _dot_tn(a, b):
    return lax.dot_general(a, b, (((0,), (0,)), ((), ())), preferred_element_type=f32)


def _resident(shape, index_map):
    return pl.BlockSpec(shape, index_map, pipeline_mode=pl.Buffered(1))


def _ada_kernel(c_ref, w_ref, b_ref, o_ref):
    s = _silu(c_ref[...]).astype(bf16)
    o_ref[0] = _dot(s, w_ref[0].astype(bf16)) + b_ref[0]


def _ada_modulation(cc, w_ada, b_ada):
    depth, d, n = w_ada.shape
    tn = 1024
    return pl.pallas_call(
        _ada_kernel,
        grid=(depth, n // tn),
        in_specs=[
            pl.BlockSpec((8, d), lambda l, j: (0, 0)),
            pl.BlockSpec((1, d, tn), lambda l, j: (l, 0, j)),
            pl.BlockSpec((1, 1, tn), lambda l, j: (l, 0, j)),
        ],
        out_specs=pl.BlockSpec((1, 8, tn), lambda l, j: (l, 0, j)),
        out_shape=jax.ShapeDtypeStruct((depth, 8, n), f32),
        compiler_params=_cparams(("arbitrary", "arbitrary")),
        name="ada_modulation",
    )(cc, w_ada, b_ada.reshape(depth, 1, n))


def _rms_mod(x, gain, shift, scale):
    y = x * lax.rsqrt(jnp.mean(x * x, axis=-1, keepdims=True) + EPS)
    return (y * gain) * (1.0 + scale) + shift


def _inproj_kernel(x_ref, mod_ref, gain_ref, wqkv_ref, wz_ref, wxbc_ref, wdt_ref, wdtT_ref,
                   cos_ref, sina_ref, sinb_ref, qkv_ref, z_ref, xbc_ref, dt_ref, dtT_ref):
    d = D_MODEL
    mod = mod_ref[0]
    h = _rms_mod(x_ref[...], gain_ref[...], mod[:, 0:d], mod[:, d:2 * d]).astype(bf16)
    acc = _dot(h, wqkv_ref[...])
    qkv_ref[:, 0:ROPE_LO] = acc[:, 0:ROPE_LO].astype(bf16)
    cos, sina, sinb = cos_ref[...], sina_ref[...], sinb_ref[...]
    quarter = HEAD_DIM // 4
    for c0 in range(ROPE_LO, ROPE_HI, HEAD_DIM):
        xh = acc[:, c0:c0 + HEAD_DIM]
        rot = (xh * cos + pltpu.roll(xh, HEAD_DIM - quarter, 1) * sina + pltpu.roll(xh, quarter, 1) * sinb)
        qkv_ref[:, c0:c0 + HEAD_DIM] = rot.astype(bf16)
    qkv_ref[:, ROPE_HI:QKV_WIDTH] = acc[:, ROPE_HI:QKV_WIDTH].astype(bf16)
    z_ref[...] = _dot(h, wz_ref[...])
    xbc_ref[...] = _dot(h, wxbc_ref[...])
    dt_ref[...] = _dot(h, wdt_ref[...])
    dtT_ref[...] = _dot_nt(wdtT_ref[...], h)


def _in_projection(xa, mod4, gain, w_in, rope_tabs, t_per_batch):
    m, d = xa.shape
    tm = ROW_BLOCK
    nblk = t_per_batch // tm
    wb = w_in.astype(bf16)
    wqkv = wb[:, 0:QKV_WIDTH]
    wz = wb[:, Z_LO:XBC_LO]
    wxbc = wb[:, XBC_LO:DT_LO]
    wdt = wb[:, DT_LO:DT_LO + DT_WIDTH]
    wdtT = wdt.T
    cos, sina, sinb = rope_tabs
    row = lambda i: (i, 0)
    const = lambda i: (0, 0)
    mod_row = lambda i: (jnp.where(i % nblk == 0, 2, i // nblk), 0, 0)
    tab_row = lambda i: (i % nblk, 0)
    return pl.pallas_call(
        _inproj_kernel,
        grid=(m // tm,),
        in_specs=[
            pl.BlockSpec((tm, d), row),
            pl.BlockSpec((1, 1, 6 * d), mod_row),
            _resident((1, d), const),
            _resident((d, QKV_WIDTH), const),
            _resident((d, SSD_D_INNER), const),
            _resident((d, SSD_XBC_WIDTH), const),
            _resident((d, DT_WIDTH), const),
            _resident((DT_WIDTH, d), const),
            pl.BlockSpec((tm, HEAD_DIM), tab_row),
            pl.BlockSpec((tm, HEAD_DIM), tab_row),
            pl.BlockSpec((tm, HEAD_DIM), tab_row),
        ],
        out_specs=[
            pl.BlockSpec((tm, QKV_WIDTH), row),
            pl.BlockSpec((tm, SSD_D_INNER), row),
            pl.BlockSpec((tm, SSD_XBC_WIDTH), row),
            pl.BlockSpec((tm, DT_WIDTH), row),
            pl.BlockSpec((DT_WIDTH, tm), lambda i: (0, i)),
        ],
        out_shape=[
            jax.ShapeDtypeStruct((m, QKV_WIDTH), bf16),
            jax.ShapeDtypeStruct((m, SSD_D_INNER), f32),
            jax.ShapeDtypeStruct((m, SSD_XBC_WIDTH), f32),
            jax.ShapeDtypeStruct((m, DT_WIDTH), f32),
            jax.ShapeDtypeStruct((DT_WIDTH, m), f32),
        ],
        compiler_params=_cparams(("arbitrary",)),
        name="in_projection",
    )(xa, mod4, gain.reshape(1, d), wqkv, wz, wxbc, wdt, wdtT, cos, sina, sinb)


def _rope_tables(seq):
    t = jnp.arange(seq, dtype=i32)
    q = HEAD_DIM // 4
    inv = jnp.power(ROPE_BASE, -jnp.arange(q, dtype=f32) / q)
    ang_r = (t // GRID_W).astype(f32)[:, None] * inv
    ang_c = (t % GRID_W).astype(f32)[:, None] * inv
    zero = jnp.zeros_like(ang_r)
    cos = jnp.concatenate([jnp.cos(ang_r), jnp.cos(ang_r), jnp.cos(ang_c), jnp.cos(ang_c)], axis=-1)
    sina = jnp.concatenate([-jnp.sin(ang_r), zero, -jnp.sin(ang_c), zero], axis=-1)
    sinb = jnp.concatenate([zero, jnp.sin(ang_r), zero, jnp.sin(ang_c)], axis=-1)
    pad = lambda a, v: jnp.concatenate([jnp.full((CTX_LEN, HEAD_DIM), v, f32), a], axis=0)
    return pad(cos, 1.0), pad(sina, 0.0), pad(sinb, 0.0)


def _na_bias_tables(rpb, rows):
    col = jnp.arange(GRID_W)
    c0 = jnp.clip(col - NA_WIN_C // 2, 0, GRID_W - NA_WIN_C)
    col_ok = (col[None, :] >= c0[:, None]) & (col[None, :] < c0[:, None] + NA_WIN_C)
    dc = jnp.clip(col[None, :] - col[:, None] + NA_WIN_C - 1, 0, 2 * NA_WIN_C - 2)
    tabs = []
    for r_first in (NA_GROUP_ROWS, 0, rows - NA_GROUP_ROWS):
        start = min(max(r_first - NA_WIN_R // 2, 0), rows - NA_KEY_ROWS)
        r = r_first + jnp.arange(NA_GROUP_ROWS)
        kr = start + jnp.arange(NA_KEY_ROWS)
        r0 = jnp.clip(r - NA_WIN_R // 2, 0, rows - NA_WIN_R)
        row_ok = (kr[None, :] >= r0[:, None]) & (kr[None, :] < r0[:, None] + NA_WIN_R)
        dr = jnp.clip(kr[None, :] - r[:, None] + NA_WIN_R - 1, 0, 2 * NA_WIN_R - 2)
        b = rpb.astype(f32)[:, dr[:, None, :, None], dc[None, :, None, :]]
        ok = row_ok[:, None, :, None] & col_ok[None, :, None, :]
        b = jnp.where(ok[None], b, MASK_VALUE)
        tabs.append(b.reshape(rpb.shape[0], NA_GROUP_ROWS * GRID_W, NA_KEY_ROWS * GRID_W))
    return jnp.stack(tabs, axis=1)


def _na_kernel(q_ref, k_ref, v_ref, bias_ref, o_ref, *, rows):
    g = pl.program_id(2)
    scale = HEAD_DIM ** -0.5
    q = q_ref[...]
    kc = k_ref[0:CTX_LEN, :]
    vc = v_ref[0:CTX_LEN, :]
    s_ctx = _dot_nt(q, kc) * scale

    @pl.when(g == 0)
    def _():
        m = jnp.max(s_ctx, axis=-1, keepdims=True)
        p = jnp.exp(s_ctx - m)
        l = jnp.sum(p, axis=-1, keepdims=True)
        o_ref[...] = (_dot(p.astype(bf16), vc) / l).astype(bf16)

    @pl.when(g > 0)
    def _():
        r_first = (g - 1) * NA_GROUP_ROWS
        start = jnp.clip(r_first - NA_WIN_R // 2, 0, rows - NA_KEY_ROWS)
        off = pl.multiple_of(CTX_LEN + start * GRID_W, GRID_W)
        nk = NA_KEY_ROWS * GRID_W
        kl = k_ref[pl.ds(off, nk), :]
        vl = v_ref[pl.ds(off, nk), :]
        s_loc = _dot_nt(q, kl) * scale + bias_ref[0, 0]
        m = jnp.maximum(jnp.max(s_loc, axis=-1, keepdims=True), jnp.max(s_ctx, axis=-1, keepdims=True))
        p_loc = jnp.exp(s_loc - m)
        p_ctx = jnp.exp(s_ctx - m)
        l = jnp.sum(p_loc, axis=-1, keepdims=True) + jnp.sum(p_ctx, axis=-1, keepdims=True)
        o = _dot(p_loc.astype(bf16), vl) + _dot(p_ctx.astype(bf16), vc)
        o_ref[...] = (o / l).astype(bf16)


def _neighbourhood_attention(qkv, bias_tabs, batch, t_per_batch):
    m = qkv.shape[0]
    rows = (t_per_batch - CTX_LEN) // GRID_W
    tq = NA_GROUP_ROWS * GRID_W
    assert tq == CTX_LEN and rows % NA_GROUP_ROWS == 0 and rows >= NA_KEY_ROWS + NA_GROUP_ROWS
    ng = rows // NA_GROUP_ROWS
    nblk = t_per_batch // tq
    kcol = NA_WIDTH // HEAD_DIM
    bias_idx = lambda b, h, g: (h, jnp.where(g == 1, 1, jnp.where(g == ng, 2, 0)), 0, 0)
    return pl.pallas_call(
        functools.partial(_na_kernel, rows=rows),
        grid=(batch, NA_HEADS, ng + 1),
        in_specs=[
            pl.BlockSpec((tq, HEAD_DIM), lambda b, h, g: (b * nblk + g, h)),
            pl.BlockSpec((t_per_batch, HEAD_DIM), lambda b, h, g: (b, kcol + h)),
            pl.BlockSpec((t_per_batch, HEAD_DIM), lambda b, h, g: (b, 2 * kcol + h)),
            pl.BlockSpec((1, 1, tq, NA_KEY_ROWS * GRID_W), bias_idx),
        ],
        out_specs=pl.BlockSpec((tq, HEAD_DIM), lambda b, h, g: (b * nblk + g, h)),
        out_shape=jax.ShapeDtypeStruct((m, NA_WIDTH), bf16),
        compiler_params=_cparams(("arbitrary", "arbitrary", "arbitrary")),
        name="neighbourhood_attention",
    )(qkv, qkv, qkv, bias_tabs)


def _wa_kernel(sink_ref, q_ref, k_ref, v_ref, o_ref, *, nb):
    kh = pl.program_id(1)
    j = pl.program_id(2)
    scale = HEAD_DIM ** -0.5
    tb = WA_BLOCK
    g = WA_HEADS // WA_KV_HEADS
    q2 = jnp.concatenate([q_ref[:, i * HEAD_DIM:(i + 1) * HEAD_DIM] for i in range(g)], axis=0)
    kc = k_ref[0:CTX_LEN, :]
    vc = v_ref[0:CTX_LEN, :]
    s_ctx = _dot_nt(q2, kc) * scale
    rowi = lax.broadcasted_iota(i32, (g * tb, 1), 0)
    sink = jnp.full((g * tb, 1), sink_ref[kh * g], f32)
    for i in range(1, g):
        sink = jnp.where(rowi >= i * tb, sink_ref[kh * g + i], sink)

    def finish(o):
        for i in range(g):
            o_ref[:, i * HEAD_DIM:(i + 1) * HEAD_DIM] = o[i * tb:(i + 1) * tb].astype(bf16)

    @pl.when(j < CTX_LEN // tb)
    def _():
        m = jnp.maximum(jnp.max(s_ctx, axis=-1, keepdims=True), sink)
        p = jnp.exp(s_ctx - m)
        l = jnp.sum(p, axis=-1, keepdims=True) + jnp.exp(sink - m)
        finish(_dot(p.astype(bf16), vc) / l)

    @pl.when(j >= CTX_LEN // tb)
    def _():
        n = j - CTX_LEN // tb
        nk = tb + 2 * WA_WINDOW
        st = jnp.clip(n * tb - WA_WINDOW, 0, nb * tb - nk)
        off = pl.multiple_of(CTX_LEN + st, tb)
        kl = k_ref[pl.ds(off, nk), :]
        vl = v_ref[pl.ds(off, nk), :]
        qpos = n * tb + lax.broadcasted_iota(i32, (g * tb, nk), 0) % tb
        kpos = st + lax.broadcasted_iota(i32, (g * tb, nk), 1)
        ok = jnp.abs(qpos - kpos) <= WA_WINDOW
        s_loc = jnp.where(ok, _dot_nt(q2, kl) * scale, MASK_VALUE)
        m = jnp.maximum(jnp.maximum(jnp.max(s_loc, axis=-1, keepdims=True), jnp.max(s_ctx, axis=-1, keepdims=True)), sink)
        p_loc = jnp.exp(s_loc - m)
        p_ctx = jnp.exp(s_ctx - m)
        l = jnp.sum(p_loc, axis=-1, keepdims=True) + jnp.sum(p_ctx, axis=-1, keepdims=True) + jnp.exp(sink - m)
        finish((_dot(p_loc.astype(bf16), vl) + _dot(p_ctx.astype(bf16), vc)) / l)


def _window_attention(qkv, sink, batch, t_per_batch):
    m = qkv.shape[0]
    tb = WA_BLOCK
    nb = (t_per_batch - CTX_LEN) // tb
    assert nb * tb >= tb + 2 * WA_WINDOW
    nblk = t_per_batch // tb
    g = WA_HEADS // WA_KV_HEADS
    qcol = ROPE_LO // (g * HEAD_DIM)
    kcol = (ROPE_LO + WA_WIDTH) // HEAD_DIM
    vcol = kcol + WA_KV_HEADS
    return pl.pallas_call(
        functools.partial(_wa_kernel, nb=nb),
        grid=(batch, WA_KV_HEADS, nblk),
        in_specs=[
            pl.BlockSpec(memory_space=pltpu.SMEM),
            pl.BlockSpec((tb, g * HEAD_DIM), lambda b, kh, j: (b * nblk + j, qcol + kh)),
            pl.BlockSpec((t_per_batch, HEAD_DIM), lambda b, kh, j: (b, kcol + kh)),
            pl.BlockSpec((t_per_batch, HEAD_DIM), lambda b, kh, j: (b, vcol + kh)),
        ],
        out_specs=pl.BlockSpec((tb, g * HEAD_DIM), lambda b, kh, j: (b * nblk + j, kh)),
        out_shape=jax.ShapeDtypeStruct((m, WA_WIDTH), bf16),
        compiler_params=_cparams(("arbitrary", "arbitrary", "arbitrary")),
        name="window_attention",
    )(sink.astype(f32), qkv, qkv, qkv)


def _conv_kernel(xp_ref, xc_ref, xn_ref, w_ref, b_ref, xs_ref, bc_ref, *, nblk):
    j = pl.program_id(0) % nblk
    tm = xc_ref.shape[0]
    prev_ok = jnp.where(j >= 2, 1.0, 0.0)
    next_ok = jnp.where((j >= 1) & (j <= nblk - 2), 1.0, 0.0)
    ext = jnp.concatenate([xp_ref[...] * prev_ok, xc_ref[...], xn_ref[...] * next_ok], axis=0)
    n_ext = tm + 2 * CONV_HALO
    acc = jnp.zeros((tm, SSD_XBC_WIDTH), f32) + b_ref[...]
    for k in range(SSD_CONV):
        shifted = pltpu.roll(ext, (SSD_CONV // 2 - k) % n_ext, 0)[CONV_HALO:CONV_HALO + tm]
        acc = acc + shifted * w_ref[k:k + 1, :]
    y = _silu(acc)
    xs_ref[...] = y[:, 0:SSD_D_INNER]
    bc_ref[...] = y[:, SSD_D_INNER:].astype(bf16)


def _ssd_conv(xbc, conv_w, conv_b, t_per_batch):
    m = xbc.shape[0]
    tm = ROW_BLOCK
    nblk = t_per_batch // tm
    hb = tm // CONV_HALO
    last = m // CONV_HALO - 1
    return pl.pallas_call(
        functools.partial(_conv_kernel, nblk=nblk),
        grid=(m // tm,),
        in_specs=[
            pl.BlockSpec((CONV_HALO, SSD_XBC_WIDTH), lambda i: (jnp.maximum(i * hb - 1, 0), 0)),
            pl.BlockSpec((tm, SSD_XBC_WIDTH), lambda i: (i, 0)),
            pl.BlockSpec((CONV_HALO, SSD_XBC_WIDTH), lambda i: (jnp.minimum((i + 1) * hb, last), 0)),
            pl.BlockSpec((SSD_CONV, SSD_XBC_WIDTH), lambda i: (0, 0)),
            pl.BlockSpec((1, SSD_XBC_WIDTH), lambda i: (0, 0)),
        ],
        out_specs=[
            pl.BlockSpec((tm, SSD_D_INNER), lambda i: (i, 0)),
            pl.BlockSpec((tm, SSD_BC_WIDTH), lambda i: (i, 0)),
        ],
        out_shape=[
            jax.ShapeDtypeStruct((m, SSD_D_INNER), f32),
            jax.ShapeDtypeStruct((m, SSD_BC_WIDTH), bf16),
        ],
        compiler_params=_cparams(("arbitrary",)),
        name="ssd_conv",
    )(xbc, xbc, xbc, conv_w, conv_b.reshape(1, -1))


def _expand_heads(small, e_ref):
    hi = small.astype(bf16)
    lo = (small - hi.astype(f32)).astype(bf16)
    e = e_ref[...]
    return _dot(hi, e) + _dot(lo, e)


def _ssd_kernel(xs_ref, bc_ref, dt_ref, dtT_ref, alog_ref, alogc_ref, dtb_ref, dtbc_ref, dsk_ref, e_ref,
                y_ref, state_ref):
    dirn = pl.program_id(1)
    j = pl.program_id(2)
    lc = SSD_CHUNK
    nh = SSD_HEADS
    gw = SSD_GROUP_WIDTH
    fwd = dirn == 0

    @pl.when(j == 0)
    def _():
        state_ref[...] = jnp.zeros_like(state_ref)

    dt_all = _softplus(dt_ref[...] + dtb_ref[...])
    dt = jnp.where(fwd, dt_all[:, 0:nh], dt_all[:, nh:2 * nh])
    dtT_all = _softplus(dtT_ref[...] + dtbc_ref[...])
    dtT = jnp.where(fwd, dtT_all[0:nh], dtT_all[nh:2 * nh])
    a_row = -jnp.exp(jnp.where(fwd, alog_ref[0:1, :], alog_ref[1:2, :]))
    a_col = -jnp.exp(jnp.where(fwd, alogc_ref[0], alogc_ref[1]))
    d_row = jnp.where(fwd, dsk_ref[0:1, :], dsk_ref[1:2, :])
    da = dt * a_row
    daT = dtT * a_col
    ri = lax.broadcasted_iota(i32, (lc, lc), 0)
    ci = lax.broadcasted_iota(i32, (lc, lc), 1)
    tri = jnp.where(fwd, ri - ci, ci - ri) >= 0
    trif = tri.astype(f32)
    acs = jnp.dot(trif, da, preferred_element_type=f32, precision=lax.Precision.HIGHEST)
    acsT = lax.dot_general(daT, trif, (((1,), (1,)), ((), ())), preferred_element_type=f32,
                           precision=lax.Precision.HIGHEST)
    tot = jnp.sum(da, axis=0, keepdims=True)
    small = jnp.concatenate([jnp.exp(acs), jnp.exp(tot - acs) * dt,
                             jnp.broadcast_to(jnp.exp(tot), (8, nh)), jnp.broadcast_to(d_row, (8, nh))], axis=0)
    big = _expand_heads(small, e_ref)
    eacs_x = big[0:lc]
    w_x = big[lc:2 * lc]
    dec_x = big[2 * lc:2 * lc + 1]
    dsk_x = big[2 * lc + 8:2 * lc + 9]

    x = xs_ref[...]
    xw = (x * w_x).astype(bf16)
    xb = x.astype(bf16)
    for g in range(SSD_GROUPS):
        lo = g * gw
        bg = bc_ref[:, g * SSD_D_STATE:(g + 1) * SSD_D_STATE]
        cg = bc_ref[:, (SSD_GROUPS + g) * SSD_D_STATE:(SSD_GROUPS + g + 1) * SSD_D_STATE]
        cb = _dot_nt(cg, bg)
        st = state_ref[g]
        y_inter = _dot(cg, st.astype(bf16))
        state_ref[g] = st * dec_x[:, lo:lo + gw] + _dot_tn(bg, xw[:, lo:lo + gw])
        ys = []
        for k in range(SSD_HEADS_PER_GROUP):
            h = g * SSD_HEADS_PER_GROUP + k
            seg = jnp.exp(jnp.where(tri, acs[:, h:h + 1] - acsT[h:h + 1, :], -jnp.inf))
            mat = (cb * seg * dtT[h:h + 1, :]).astype(bf16)
            ys.append(_dot(mat, xb[:, h * SSD_HEAD_DIM:(h + 1) * SSD_HEAD_DIM]))
        y_g = jnp.concatenate(ys, axis=-1) + y_inter * eacs_x[:, lo:lo + gw] + x[:, lo:lo + gw] * dsk_x[:, lo:lo + gw]
        y_ref[0, :, lo:lo + gw] = y_g


def _ssd_scan(xs, bc, dt_raw, dtT_raw, a_log, dt_bias, d_skip, batch, t_per_batch):
    m = xs.shape[0]
    lc = SSD_CHUNK
    nch = t_per_batch // lc
    nctx = CTX_LEN // lc

    def chunk(b, d, j):
        rev = jnp.where(j < nctx, nctx - 1 - j, nch - 1 + nctx - j)
        return b * nch + jnp.where(d == 0, j, rev)

    expand = (jnp.arange(SSD_D_INNER)[None, :] // SSD_HEAD_DIM == jnp.arange(SSD_HEADS)[:, None]).astype(bf16)
    const2 = lambda b, d, j: (0, 0)
    return pl.pallas_call(
        _ssd_kernel,
        grid=(batch, 2, nch),
        in_specs=[
            pl.BlockSpec((lc, SSD_D_INNER), lambda b, d, j: (chunk(b, d, j), 0)),
            pl.BlockSpec((lc, SSD_BC_WIDTH), lambda b, d, j: (chunk(b, d, j), 0)),
            pl.BlockSpec((lc, DT_WIDTH), lambda b, d, j: (chunk(b, d, j), 0)),
            pl.BlockSpec((DT_WIDTH, lc), lambda b, d, j: (0, chunk(b, d, j))),
            pl.BlockSpec((2, SSD_HEADS), const2),
            pl.BlockSpec((2, SSD_HEADS, 1), lambda b, d, j: (0, 0, 0)),
            pl.BlockSpec((1, DT_WIDTH), const2),
            pl.BlockSpec((DT_WIDTH, 1), const2),
            pl.BlockSpec((2, SSD_HEADS), const2),
            pl.BlockSpec((SSD_HEADS, SSD_D_INNER), const2),
        ],
        out_specs=pl.BlockSpec((1, lc, SSD_D_INNER), lambda b, d, j: (d, chunk(b, d, j), 0)),
        out_shape=jax.ShapeDtypeStruct((2, m, SSD_D_INNER), f32),
        scratch_shapes=[pltpu.VMEM((SSD_GROUPS, SSD_D_STATE, SSD_GROUP_WIDTH), f32)],
        compiler_params=_cparams(("arbitrary", "arbitrary", "arbitrary")),
        name="ssd_scan",
    )(xs, bc, dt_raw, dtT_raw, a_log.astype(f32), a_log.astype(f32).reshape(2, SSD_HEADS, 1),
      dt_bias.astype(f32).reshape(1, DT_WIDTH), dt_bias.astype(f32).reshape(DT_WIDTH, 1), d_skip.astype(f32), expand)


def _outproj_kernel(oa_ref, ob_ref, y_ref, z_ref, sgain_ref, w_ref, x_ref, mod_ref, ngain_ref, xmid_ref, hf_ref):
    d = D_MODEL
    u = (y_ref[0] + y_ref[1]) * _silu(z_ref[...])
    parts = []
    for g in range(SSD_GROUPS):
        ug = u[:, g * SSD_GROUP_WIDTH:(g + 1) * SSD_GROUP_WIDTH]
        parts.append(ug * lax.rsqrt(jnp.mean(ug * ug, axis=-1, keepdims=True) + EPS))
    gn = (jnp.concatenate(parts, axis=-1) * sgain_ref[...]).astype(bf16)
    acc = (_dot(oa_ref[...], w_ref[0:NA_WIDTH, :]) + _dot(ob_ref[...], w_ref[NA_WIDTH:NA_WIDTH + WA_WIDTH, :])
           + _dot(gn, w_ref[NA_WIDTH + WA_WIDTH:, :]))
    mod = mod_ref[0]
    xm = x_ref[...] + mod[:, 2 * d:3 * d] * acc
    xmid_ref[...] = xm
    hf_ref[...] = _rms_mod(xm, ngain_ref[...], mod[:, 3 * d:4 * d], mod[:, 4 * d:5 * d])


def _out_projection(o_a, o_b, y2, z, ssd_norm, w_out, xa, mod4, ngain, t_per_batch):
    m, d = xa.shape
    tm = ROW_BLOCK
    nblk = t_per_batch // tm
    row = lambda i: (i, 0)
    const = lambda i: (0, 0)
    mod_row = lambda i: (jnp.where(i % nblk == 0, 2, i // nblk), 0, 0)
    return pl.pallas_call(
        _outproj_kernel,
        grid=(m // tm,),
        in_specs=[
            pl.BlockSpec((tm, NA_WIDTH), row),
            pl.BlockSpec((tm, WA_WIDTH), row),
            pl.BlockSpec((2, tm, SSD_D_INNER), lambda i: (0, i, 0)),
            pl.BlockSpec((tm, SSD_D_INNER), row),
            _resident((1, SSD_D_INNER), const),
            _resident((d, d), const),
            pl.BlockSpec((tm, d), row),
            pl.BlockSpec((1, 1, 6 * d), mod_row),
            _resident((1, d), const),
        ],
        out_specs=[pl.BlockSpec((tm, d), row), pl.BlockSpec((tm, d), row)],
        out_shape=[jax.ShapeDtypeStruct((m, d), f32), jax.ShapeDtypeStruct((m, d), f32)],
        compiler_params=_cparams(("arbitrary",)),
        name="out_projection",
    )(o_a, o_b, y2, z, ssd_norm.reshape(1, -1), w_out.astype(bf16), xa, mod4, ngain.reshape(1, d))


def _router_kernel(hf_ref, wrT_ref, rb_ref, e_ref, gate_ref, rank_ref, cnt_ref, carry_ref):
    i = pl.program_id(0)
    tm = hf_ref.shape[0]
    ng, ge = N_EXPERT_GROUPS, EXPERTS_PER_GROUP

    @pl.when(i == 0)
    def _():
        carry_ref[...] = jnp.zeros_like(carry_ref)

    aff = _sigmoid(_dot_nt(wrT_ref[...], hf_ref[...].astype(bf16)))
    sel3 = (aff + rb_ref[...]).reshape(ng, ge, tm)
    aff3 = aff.reshape(ng, ge, tm)
    io = lax.broadcasted_iota(i32, (ng, ge, tm), 1)
    m1 = jnp.max(sel3, axis=1, keepdims=True)
    i1 = jnp.min(jnp.where(sel3 == m1, io, ge), axis=1, keepdims=True)
    rest = jnp.where(io == i1, -jnp.inf, sel3)
    m2 = jnp.max(rest, axis=1, keepdims=True)
    i2 = jnp.min(jnp.where(rest == m2, io, ge), axis=1, keepdims=True)
    a1 = jnp.sum(jnp.where(io == i1, aff3, 0.0), axis=1)
    a2 = jnp.sum(jnp.where(io == i2, aff3, 0.0), axis=1)
    score = (m1 + m2)[:, 0, :]
    gi = lax.broadcasted_iota(i32, (ng, tm), 0)
    best = jnp.max(score, axis=0, keepdims=True)
    gb = jnp.min(jnp.where(score == best, gi, ng), axis=0, keepdims=True)
    picked = gi == gb
    pick_i = lambda a: jnp.sum(jnp.where(picked, a, 0), axis=0, keepdims=True)
    pick_f = lambda a: jnp.sum(jnp.where(picked, a, 0.0), axis=0, keepdims=True)
    e1 = gb * ge + pick_i(i1[:, 0, :])
    e2 = gb * ge + pick_i(i2[:, 0, :])
    g1 = pick_f(a1)
    g2 = pick_f(a2)
    den = g1 + g2
    e_ref[...] = jnp.concatenate([e1, e2], axis=0)
    gate_ref[...] = jnp.concatenate([g1 / den, g2 / den], axis=0)

    ei = lax.broadcasted_iota(i32, (N_EXPERTS, tm), 0)
    o1 = jnp.where(ei == e1, 1.0, 0.0)
    o2 = jnp.where(ei == e2, 1.0, 0.0)
    both = o1 + o2
    upper = jnp.where(lax.broadcasted_iota(i32, (tm, tm), 0) <= lax.broadcasted_iota(i32, (tm, tm), 1), 1.0, 0.0)
    incl = _dot(both.astype(bf16), upper.astype(bf16))
    before = incl - both + carry_ref[...]
    r1 = jnp.sum(o1 * before, axis=0, keepdims=True)
    r2 = jnp.sum(o2 * before, axis=0, keepdims=True)
    rank_ref[...] = jnp.concatenate([r1, r2], axis=0).astype(i32)
    total = carry_ref[...] + jnp.sum(both, axis=1, keepdims=True)
    carry_ref[...] = total
    cnt_ref[...] = jnp.broadcast_to(total, cnt_ref.shape).astype(i32)


def _router(hf, w_router, router_bias):
    m, d = hf.shape
    tm = ROW_BLOCK
    return pl.pallas_call(
        _router_kernel,
        grid=(m // tm,),
        in_specs=[
            pl.BlockSpec((tm, d), lambda i: (i, 0)),
            pl.BlockSpec((N_EXPERTS, d), lambda i: (0, 0)),
            pl.BlockSpec((N_EXPERTS, 1), lambda i: (0, 0)),
        ],
        out_specs=[
            pl.BlockSpec((2, tm), lambda i: (0, i)),
            pl.BlockSpec((2, tm), lambda i: (0, i)),
            pl.BlockSpec((2, tm), lambda i: (0, i)),
            pl.BlockSpec((N_EXPERTS, 128), lambda i: (0, 0)),
        ],
        out_shape=[
            jax.ShapeDtypeStruct((2, m), i32),
            jax.ShapeDtypeStruct((2, m), f32),
            jax.ShapeDtypeStruct((2, m), i32),
            jax.ShapeDtypeStruct((N_EXPERTS, 128), i32),
        ],
        scratch_shapes=[pltpu.VMEM((N_EXPERTS, 1), f32)],
        compiler_params=_cparams(("arbitrary",)),
        name="router",
    )(hf, w_router.T.astype(bf16), router_bias.astype(f32).reshape(N_EXPERTS, 1))


def _dispatch_kernel(slot_ref, hf_ref, xin_ref, xbuf_ref, sem):
    del xin_ref
    i = pl.program_id(0)
    tm = slot_ref.shape[1]

    def row_copy(t, s):
        return pltpu.make_async_copy(hf_ref.at[pl.ds(t, 1)], xbuf_ref.at[pl.ds(s, 1)], sem)

    def issue(r, carry):
        for k in range(2):
            row_copy(i * tm + r, slot_ref[k, r]).start()
        return carry

    lax.fori_loop(0, tm, issue, 0)

    def drain(r, carry):
        for k in range(2):
            row_copy(0, 0).wait()
        return carry

    lax.fori_loop(0, tm, drain, 0)


def _dispatch(hf, slot, n_pad):
    m, d = hf.shape
    tm = ROW_BLOCK
    xzero = jnp.zeros((n_pad, d), f32)
    return pl.pallas_call(
        _dispatch_kernel,
        grid=(m // tm,),
        in_specs=[
            pl.BlockSpec((2, tm), lambda i: (0, i), memory_space=pltpu.SMEM),
            pl.BlockSpec(memory_space=pl.ANY),
            pl.BlockSpec(memory_space=pl.ANY),
        ],
        out_specs=pl.BlockSpec(memory_space=pl.ANY),
        out_shape=jax.ShapeDtypeStruct((n_pad, d), f32),
        scratch_shapes=[pltpu.SemaphoreType.DMA],
        input_output_aliases={2: 0},
        compiler_params=_cparams(("arbitrary",)),
        name="moe_dispatch",
    )(slot, hf, xzero)


def _moe_kernel(be_ref, nb_ref, x_ref, wg_ref, wu_ref, wd_ref, y_ref):
    del be_ref
    used = pl.program_id(0) < nb_ref[0]

    @pl.when(used)
    def _():
        x = x_ref[...].astype(bf16)
        g = _dot(x, wg_ref[0])
        u = _dot(x, wu_ref[0])
        y_ref[...] = _dot((_silu(g) * u).astype(bf16), wd_ref[0])

    @pl.when(jnp.logical_not(used))
    def _():
        y_ref[...] = jnp.zeros_like(y_ref)


def _moe_experts(xbuf, block_expert, n_blocks_used, wg, wu, wd):
    n_pad, d = xbuf.shape
    tme = MOE_BLOCK
    f = wg.shape[2]
    blk = lambda i, be, nb: (jnp.minimum(i, nb[0] - 1), 0)
    wsel = lambda i, be, nb: (be[i], 0, 0)
    return pl.pallas_call(
        _moe_kernel,
        grid_spec=pltpu.PrefetchScalarGridSpec(
            num_scalar_prefetch=2,
            grid=(n_pad // tme,),
            in_specs=[
                pl.BlockSpec((tme, d), blk),
                pl.BlockSpec((1, d, f), wsel),
                pl.BlockSpec((1, d, f), wsel),
                pl.BlockSpec((1, f, d), wsel),
            ],
            out_specs=pl.BlockSpec((tme, d), lambda i, be, nb: (i, 0)),
        ),
        out_shape=jax.ShapeDtypeStruct((n_pad, d), f32),
        compiler_params=_cparams(("arbitrary",)),
        name="moe_experts",
    )(block_expert, n_blocks_used, xbuf, wg, wu, wd)


def _combine_kernel(slot_ref, ybuf_ref, gate_ref, x_ref, mod_ref, fgain_ref, o_ref, buf_ref, sem, *, final_norm):
    tm = x_ref.shape[0]
    d = D_MODEL

    def row_copy(k, r, s):
        return pltpu.make_async_copy(ybuf_ref.at[pl.ds(s, 1)], buf_ref.at[k, pl.ds(r, 1)], sem)

    def issue(r, carry):
        for k in range(2):
            row_copy(k, r, slot_ref[k, r]).start()
        return carry

    lax.fori_loop(0, tm, issue, 0)

    def drain(r, carry):
        for k in range(2):
            row_copy(k, r, 0).wait()
        return carry

    lax.fori_loop(0, tm, drain, 0)
    gate = gate_ref[...]
    y = buf_ref[0] * gate[:, 0:1] + buf_ref[1] * gate[:, 1:2]
    xn = x_ref[...] + mod_ref[0][:, 5 * d:6 * d] * y
    if final_norm:
        xn = (xn * lax.rsqrt(jnp.mean(xn * xn, axis=-1, keepdims=True) + EPS)) * fgain_ref[...]
    o_ref[...] = xn


def _combine(ybuf, slot, gate_t, xmid, mod4, fgain, batch, t_per_batch, latent_only):
    m, d = xmid.shape
    tm = ROW_BLOCK
    nblk = t_per_batch // tm
    if latent_only:
        grid = (batch, nblk - 1)
        src = lambda b, j: b * nblk + 1 + j
        dst = lambda b, j: b * (nblk - 1) + j
        modr = lambda b, j: b
        m_out = batch * (t_per_batch - CTX_LEN)
    else:
        grid = (batch, nblk)
        src = dst = lambda b, j: b * nblk + j
        modr = lambda b, j: jnp.where(j == 0, 2, b)
        m_out = m
    return pl.pallas_call(
        functools.partial(_combine_kernel, final_norm=latent_only),
        grid=grid,
        in_specs=[
            pl.BlockSpec((2, tm), lambda b, j: (0, src(b, j)), memory_space=pltpu.SMEM),
            pl.BlockSpec(memory_space=pl.ANY),
            pl.BlockSpec((tm, 2), lambda b, j: (src(b, j), 0)),
            pl.BlockSpec((tm, d), lambda b, j: (src(b, j), 0)),
            pl.BlockSpec((1, 1, 6 * d), lambda b, j: (modr(b, j), 0, 0)),
            pl.BlockSpec((1, d), lambda b, j: (0, 0)),
        ],
        out_specs=pl.BlockSpec((tm, d), lambda b, j: (dst(b, j), 0)),
        out_shape=jax.ShapeDtypeStruct((m_out, d), f32),
        scratch_shapes=[pltpu.VMEM((2, tm, d), f32), pltpu.SemaphoreType.DMA],
        compiler_params=_cparams(("arbitrary", "arbitrary")),
        name="moe_combine",
    )(slot, ybuf, gate_t, xmid, mod4, fgain.reshape(1, d))


def _moe_plan(expert, rank, counts, n_assign):
    tme = MOE_BLOCK
    n_blocks = -(-n_assign // tme) + N_EXPERTS
    blocks_per_expert = (counts + tme - 1) // tme
    block_end = jnp.cumsum(blocks_per_expert)
    row_start = (block_end - blocks_per_expert) * tme
    slot = jnp.take(row_start, expert) + rank
    block_expert = jnp.minimum(jnp.searchsorted(block_end, jnp.arange(n_blocks), side="right"), N_EXPERTS - 1)
    return slot.astype(i32), block_expert.astype(i32), block_end[-1:].astype(i32), n_blocks * tme


def kernel(x, c, ctx, c_ctx, w_ada, b_ada, norm_mix, norm_ffn, norm_final, w_in, w_out, na_rpb, wa_sink,
           ssd_conv_w, ssd_conv_b, ssd_a_log, ssd_dt_bias, ssd_d, ssd_norm, w_router, router_bias,
           w_gate, w_up, w_down):
    batch, seq, d = x.shape
    depth = w_ada.shape[0]
    assert d == D_MODEL and ctx.shape[1] == CTX_LEN and seq % ROW_BLOCK == 0 and batch <= 2
    t_per_batch = CTX_LEN + seq
    m = batch * t_per_batch
    rows = seq // GRID_W

    xa = jnp.concatenate([ctx, x], axis=1).reshape(m, d)
    cc = jnp.zeros((8, d), f32).at[0:batch].set(c).at[2].set(c_ctx)
    mod_all = _ada_modulation(cc, w_ada, b_ada).reshape(depth, 8, 1, 6 * d)
    rope_tabs = _rope_tables(seq)

    out = None
    for l in range(depth):
        mod4 = mod_all[l]
        qkv, z, xbc, dt_raw, dtT_raw = _in_projection(xa, mod4, norm_mix[l], w_in[l], rope_tabs, t_per_batch)
        o_a = _neighbourhood_attention(qkv, _na_bias_tables(na_rpb[l], rows), batch, t_per_batch)
        o_b = _window_attention(qkv, wa_sink[l], batch, t_per_batch)
        xs, bc = _ssd_conv(xbc, ssd_conv_w[l], ssd_conv_b[l], t_per_batch)
        y2 = _ssd_scan(xs, bc, dt_raw, dtT_raw, ssd_a_log[l], ssd_dt_bias[l], ssd_d[l], batch, t_per_batch)
        xmid, hf = _out_projection(o_a, o_b, y2, z, ssd_norm[l], w_out[l], xa, mod4, norm_ffn[l], t_per_batch)
        expert, gate, rank, cnt = _router(hf, w_router, router_bias)
        slot, block_expert, n_used, n_pad = _moe_plan(expert, rank, cnt[:, 0], 2 * m)
        xbuf = _dispatch(hf, slot, n_pad)
        ybuf = _moe_experts(xbuf, block_expert, n_used, w_gate[l].astype(bf16), w_up[l].astype(bf16),
                            w_down[l].astype(bf16))
        last = l == depth - 1
        res = _combine(ybuf, slot, gate.T, xmid, mod4, norm_final, batch, t_per_batch, latent_only=last)
        if last:
            out = res.reshape(batch, seq, d)
        else:
            xa = res
    return out
```

```python
import functools

import jax
import jax.numpy as jnp
import numpy as np
from jax import lax
from jax.experimental import pallas as pl
from jax.experimental.pallas import tpu as pltpu

f32 = jnp.float32
bf16 = jnp.bfloat16
i32 = jnp.int32

D_MODEL = 2048
GRID_W = 64
CTX_LEN = 256
EPS = 1e-6
MASK_VALUE = -1e30
HEAD_DIM = 128
NA_WIDTH = D_MODEL // 4
NA_HEADS = NA_WIDTH // HEAD_DIM
NA_WIN_R = 8
NA_WIN_C = 16
WA_WIDTH = D_MODEL // 4
WA_HEADS = WA_WIDTH // HEAD_DIM
WA_KV_HEADS = 2
WA_KV_WIDTH = WA_KV_HEADS * HEAD_DIM
WA_WINDOW = 128
WA_BLOCK = 128
ROPE_BASE = 10000.0
SSD_D_INNER = D_MODEL // 2
SSD_HEAD_DIM = 64
SSD_HEADS = SSD_D_INNER // SSD_HEAD_DIM
SSD_GROUPS = 2
SSD_D_STATE = 128
SSD_CONV = 5
SSD_BC_WIDTH = 2 * SSD_GROUPS * SSD_D_STATE
SSD_XBC_WIDTH = SSD_D_INNER + SSD_BC_WIDTH
SSD_GROUP_WIDTH = SSD_D_INNER // SSD_GROUPS
SSD_HEADS_PER_GROUP = SSD_HEADS // SSD_GROUPS
N_EXPERTS = 32
N_EXPERT_GROUPS = 4
EXPERTS_PER_GROUP = N_EXPERTS // N_EXPERT_GROUPS
D_FF_EXPERT = D_MODEL // 2

QKV_WIDTH = 3 * NA_WIDTH + WA_WIDTH + 2 * WA_KV_WIDTH
ROPE_LO = 3 * NA_WIDTH
ROPE_HI = ROPE_LO + WA_WIDTH + WA_KV_WIDTH
Z_LO = QKV_WIDTH
XBC_LO = Z_LO + SSD_D_INNER
DT_LO = XBC_LO + SSD_XBC_WIDTH
DT_WIDTH = 2 * SSD_HEADS

ROW_BLOCK = 256
SSD_CHUNK = 128
NA_GROUP_ROWS = 4
NA_KEY_ROWS = NA_GROUP_ROWS + NA_WIN_R
MOE_BLOCK = 256
ROW_UNROLL = 8
CONV_HALO = 8
VMEM_LIMIT = 56 * 1024 * 1024


def _cparams(sem):
    return pltpu.CompilerParams(dimension_semantics=sem, vmem_limit_bytes=VMEM_LIMIT)


def _sigmoid(x):
    return 1.0 / (1.0 + jnp.exp(-x))


def _silu(x):
    return x * _sigmoid(x)


def _softplus(x):
    return jnp.maximum(x, 0.0) + jnp.log1p(jnp.exp(-jnp.abs(x)))


def _dot(a, b):
    return jnp.dot(a, b, preferred_element_type=f32)


def _dot_nt(a, b):
    return lax.dot_general(a, b, (((1,), (1,)), ((), ())), preferred_element_type=f32)


def _dot_tn(a, b):
    return lax.dot_general(a, b, (((0,), (0,)), ((), ())), preferred_element_type=f32)


def _resident(shape, index_map):
    return pl.BlockSpec(shape, index_map, pipeline_mode=pl.Buffered(1))


def _ada_kernel(c_ref, w_ref, b_ref, o_ref):
    s = _silu(c_ref[...]).astype(bf16)
    o_ref[0] = _dot(s, w_ref[0].astype(bf16)) + b_ref[0]


def _ada_modulation(cc, w_ada, b_ada):
    depth, d, n = w_ada.shape
    tn = 1024
    return pl.pallas_call(
        _ada_kernel,
        grid=(depth, n // tn),
        in_specs=[
            pl.BlockSpec((8, d), lambda l, j: (0, 0)),
            pl.BlockSpec((1, d, tn), lambda l, j: (l, 0, j)),
            pl.BlockSpec((1, 1, tn), lambda l, j: (l, 0, j)),
        ],
        out_specs=pl.BlockSpec((1, 8, tn), lambda l, j: (l, 0, j)),
        out_shape=jax.ShapeDtypeStruct((depth, 8, n), f32),
        compiler_params=_cparams(("arbitrary", "arbitrary")),
        name="ada_modulation",
    )(cc, w_ada, b_ada.reshape(depth, 1, n))


def _rms_mod(x, gain, shift, scale):
    y = x * lax.rsqrt(jnp.mean(x * x, axis=-1, keepdims=True) + EPS)
    return (y * gain) * (1.0 + scale) + shift


def _inproj_kernel(x_ref, mod_ref, gain_ref, wqkv_ref, wz_ref, wxbc_ref, wdt_ref, wdtT_ref,
                   cos_ref, sina_ref, sinb_ref, qkv_ref, z_ref, xbc_ref, dt_ref, dtT_ref):
    d = D_MODEL
    mod = mod_ref[0]
    h = _rms_mod(x_ref[...], gain_ref[...], mod[:, 0:d], mod[:, d:2 * d]).astype(bf16)
    acc = _dot(h, wqkv_ref[...])
    qkv_ref[:, 0:ROPE_LO] = acc[:, 0:ROPE_LO].astype(bf16)
    cos, sina, sinb = cos_ref[...], sina_ref[...], sinb_ref[...]
    quarter = HEAD_DIM // 4
    for c0 in range(ROPE_LO, ROPE_HI, HEAD_DIM):
        xh = acc[:, c0:c0 + HEAD_DIM]
        rot = (xh * cos + pltpu.roll(xh, HEAD_DIM - quarter, 1) * sina + pltpu.roll(xh, quarter, 1) * sinb)
        qkv_ref[:, c0:c0 + HEAD_DIM] = rot.astype(bf16)
    qkv_ref[:, ROPE_HI:QKV_WIDTH] = acc[:, ROPE_HI:QKV_WIDTH].astype(bf16)
    z_ref[...] = _dot(h, wz_ref[...])
    xbc_ref[...] = _dot(h, wxbc_ref[...])
    dt_ref[...] = _dot(h, wdt_ref[...])
    dtT_ref[...] = _dot_nt(wdtT_ref[...], h)


def _in_projection(xa, mod4, gain, w_in, rope_tabs, t_per_batch):
    m, d = xa.shape
    tm = ROW_BLOCK
    nblk = t_per_batch // tm
    wb = w_in.astype(bf16)
    wqkv = wb[:, 0:QKV_WIDTH]
    wz = wb[:, Z_LO:XBC_LO]
    wxbc = wb[:, XBC_LO:DT_LO]
    wdt = wb[:, DT_LO:DT_LO + DT_WIDTH]
    wdtT = wdt.T
    cos, sina, sinb = rope_tabs
    row = lambda i: (i, 0)
    const = lambda i: (0, 0)
    mod_row = lambda i: (jnp.where(i % nblk == 0, 2, i // nblk), 0, 0)
    tab_row = lambda i: (i % nblk, 0)
    return pl.pallas_call(
        _inproj_kernel,
        grid=(m // tm,),
        in_specs=[
            pl.BlockSpec((tm, d), row),
            pl.BlockSpec((1, 1, 6 * d), mod_row),
            _resident((1, d), const),
            _resident((d, QKV_WIDTH), const),
            _resident((d, SSD_D_INNER), const),
            _resident((d, SSD_XBC_WIDTH), const),
            _resident((d, DT_WIDTH), const),
            _resident((DT_WIDTH, d), const),
            pl.BlockSpec((tm, HEAD_DIM), tab_row),
            pl.BlockSpec((tm, HEAD_DIM), tab_row),
            pl.BlockSpec((tm, HEAD_DIM), tab_row),
        ],
        out_specs=[
            pl.BlockSpec((tm, QKV_WIDTH), row),
            pl.BlockSpec((tm, SSD_D_INNER), row),
            pl.BlockSpec((tm, SSD_XBC_WIDTH), row),
            pl.BlockSpec((tm, DT_WIDTH), row),
            pl.BlockSpec((DT_WIDTH, tm), lambda i: (0, i)),
        ],
        out_shape=[
            jax.ShapeDtypeStruct((m, QKV_WIDTH), bf16),
            jax.ShapeDtypeStruct((m, SSD_D_INNER), f32),
            jax.ShapeDtypeStruct((m, SSD_XBC_WIDTH), f32),
            jax.ShapeDtypeStruct((m, DT_WIDTH), f32),
            jax.ShapeDtypeStruct((DT_WIDTH, m), f32),
        ],
        compiler_params=_cparams(("arbitrary",)),
        name="in_projection",
    )(xa, mod4, gain.reshape(1, d), wqkv, wz, wxbc, wdt, wdtT, cos, sina, sinb)


def _rope_tables(seq):
    t = jnp.arange(seq, dtype=i32)
    q = HEAD_DIM // 4
    inv = jnp.power(ROPE_BASE, -jnp.arange(q, dtype=f32) / q)
    ang_r = (t // GRID_W).astype(f32)[:, None] * inv
    ang_c = (t % GRID_W).astype(f32)[:, None] * inv
    zero = jnp.zeros_like(ang_r)
    cos = jnp.concatenate([jnp.cos(ang_r), jnp.cos(ang_r), jnp.cos(ang_c), jnp.cos(ang_c)], axis=-1)
    sina = jnp.concatenate([-jnp.sin(ang_r), zero, -jnp.sin(ang_c), zero], axis=-1)
    sinb = jnp.concatenate([zero, jnp.sin(ang_r), zero, jnp.sin(ang_c)], axis=-1)
    pad = lambda a, v: jnp.concatenate([jnp.full((CTX_LEN, HEAD_DIM), v, f32), a], axis=0)
    return pad(cos, 1.0), pad(sina, 0.0), pad(sinb, 0.0)


def _na_bias_tables(rpb, rows):
    col = np.arange(GRID_W)
    c0 = np.clip(col - NA_WIN_C // 2, 0, GRID_W - NA_WIN_C)
    col_ok = (col[None, :] >= c0[:, None]) & (col[None, :] < c0[:, None] + NA_WIN_C)
    dc = np.clip(col[None, :] - col[:, None] + NA_WIN_C - 1, 0, 2 * NA_WIN_C - 2)
    dc_hot = (dc[:, :, None] == np.arange(2 * NA_WIN_C - 1)).astype(np.float32)
    exact = lax.Precision.HIGHEST
    by_col = jnp.einsum("hde,qke->hdqk", rpb.astype(f32), dc_hot, precision=exact)
    tabs = []
    for r_first in (NA_GROUP_ROWS, 0, rows - NA_GROUP_ROWS):
        start = min(max(r_first - NA_WIN_R // 2, 0), rows - NA_KEY_ROWS)
        r = r_first + np.arange(NA_GROUP_ROWS)
        kr = start + np.arange(NA_KEY_ROWS)
        r0 = np.clip(r - NA_WIN_R // 2, 0, rows - NA_WIN_R)
        row_ok = (kr[None, :] >= r0[:, None]) & (kr[None, :] < r0[:, None] + NA_WIN_R)
        dr = np.clip(kr[None, :] - r[:, None] + NA_WIN_R - 1, 0, 2 * NA_WIN_R - 2)
        dr_hot = (dr[:, :, None] == np.arange(2 * NA_WIN_R - 1)).astype(np.float32)
        b = jnp.einsum("gid,hdqk->hgqik", dr_hot, by_col, precision=exact)
        ok = row_ok[:, None, :, None] & col_ok[None, :, None, :]
        b = jnp.where(ok[None], b, MASK_VALUE)
        tabs.append(b.reshape(rpb.shape[0], NA_GROUP_ROWS * GRID_W, NA_KEY_ROWS * GRID_W))
    return jnp.stack(tabs, axis=1)


def _na_kernel(q_ref, k_ref, v_ref, bias_ref, o_ref, *, rows):
    g = pl.program_id(2)
    scale = HEAD_DIM ** -0.5
    q = q_ref[...]
    kc = k_ref[0:CTX_LEN, :]
    vc = v_ref[0:CTX_LEN, :]
    s_ctx = _dot_nt(q, kc) * scale

    @pl.when(g == 0)
    def _():
        m = jnp.max(s_ctx, axis=-1, keepdims=True)
        p = jnp.exp(s_ctx - m)
        l = jnp.sum(p, axis=-1, keepdims=True)
        o_ref[...] = (_dot(p.astype(bf16), vc) / l).astype(bf16)

    @pl.when(g > 0)
    def _():
        r_first = (g - 1) * NA_GROUP_ROWS
        start = jnp.clip(r_first - NA_WIN_R // 2, 0, rows - NA_KEY_ROWS)
        off = pl.multiple_of(CTX_LEN + start * GRID_W, GRID_W)
        nk = NA_KEY_ROWS * GRID_W
        kl = k_ref[pl.ds(off, nk), :]
        vl = v_ref[pl.ds(off, nk), :]
        s_loc = _dot_nt(q, kl) * scale + bias_ref[0, 0]
        m = jnp.maximum(jnp.max(s_loc, axis=-1, keepdims=True), jnp.max(s_ctx, axis=-1, keepdims=True))
        p_loc = jnp.exp(s_loc - m)
        p_ctx = jnp.exp(s_ctx - m)
        l = jnp.sum(p_loc, axis=-1, keepdims=True) + jnp.sum(p_ctx, axis=-1, keepdims=True)
        o = _dot(p_loc.astype(bf16), vl) + _dot(p_ctx.astype(bf16), vc)
        o_ref[...] = (o / l).astype(bf16)


def _neighbourhood_attention(qkv, bias_tabs, batch, t_per_batch):
    m = qkv.shape[0]
    rows = (t_per_batch - CTX_LEN) // GRID_W
    tq = NA_GROUP_ROWS * GRID_W
    assert tq == CTX_LEN and rows % NA_GROUP_ROWS == 0 and rows >= NA_KEY_ROWS + NA_GROUP_ROWS
    ng = rows // NA_GROUP_ROWS
    nblk = t_per_batch // tq
    kcol = NA_WIDTH // HEAD_DIM
    bias_idx = lambda b, h, g: (h, jnp.where(g == 1, 1, jnp.where(g == ng, 2, 0)), 0, 0)
    return pl.pallas_call(
        functools.partial(_na_kernel, rows=rows),
        grid=(batch, NA_HEADS, ng + 1),
        in_specs=[
            pl.BlockSpec((tq, HEAD_DIM), lambda b, h, g: (b * nblk + g, h)),
            pl.BlockSpec((t_per_batch, HEAD_DIM), lambda b, h, g: (b, kcol + h)),
            pl.BlockSpec((t_per_batch, HEAD_DIM), lambda b, h, g: (b, 2 * kcol + h)),
            pl.BlockSpec((1, 1, tq, NA_KEY_ROWS * GRID_W), bias_idx),
        ],
        out_specs=pl.BlockSpec((tq, HEAD_DIM), lambda b, h, g: (b * nblk + g, h)),
        out_shape=jax.ShapeDtypeStruct((m, NA_WIDTH), bf16),
        compiler_params=_cparams(("arbitrary", "arbitrary", "arbitrary")),
        name="neighbourhood_attention",
    )(qkv, qkv, qkv, bias_tabs)


def _wa_kernel(sink_ref, q_ref, k_ref, v_ref, o_ref, *, nb):
    kh = pl.program_id(1)
    j = pl.program_id(2)
    scale = HEAD_DIM ** -0.5
    tb = WA_BLOCK
    g = WA_HEADS // WA_KV_HEADS
    q2 = jnp.concatenate([q_ref[:, i * HEAD_DIM:(i + 1) * HEAD_DIM] for i in range(g)], axis=0)
    kc = k_ref[0:CTX_LEN, :]
    vc = v_ref[0:CTX_LEN, :]
    s_ctx = _dot_nt(q2, kc) * scale
    rowi = lax.broadcasted_iota(i32, (g * tb, 1), 0)
    sink = jnp.full((g * tb, 1), sink_ref[kh * g], f32)
    for i in range(1, g):
        sink = jnp.where(rowi >= i * tb, sink_ref[kh * g + i], sink)

    def finish(o):
        for i in range(g):
            o_ref[:, i * HEAD_DIM:(i + 1) * HEAD_DIM] = o[i * tb:(i + 1) * tb].astype(bf16)

    @pl.when(j < CTX_LEN // tb)
    def _():
        m = jnp.maximum(jnp.max(s_ctx, axis=-1, keepdims=True), sink)
        p = jnp.exp(s_ctx - m)
        l = jnp.sum(p, axis=-1, keepdims=True) + jnp.exp(sink - m)
        finish(_dot(p.astype(bf16), vc) / l)

    @pl.when(j >= CTX_LEN // tb)
    def _():
        n = j - CTX_LEN // tb
        nk = tb + 2 * WA_WINDOW
        st = jnp.clip(n * tb - WA_WINDOW, 0, nb * tb - nk)
        off = pl.multiple_of(CTX_LEN + st, tb)
        kl = k_ref[pl.ds(off, nk), :]
        vl = v_ref[pl.ds(off, nk), :]
        qpos = n * tb + lax.broadcasted_iota(i32, (g * tb, nk), 0) % tb
        kpos = st + lax.broadcasted_iota(i32, (g * tb, nk), 1)
        ok = jnp.abs(qpos - kpos) <= WA_WINDOW
        s_loc = jnp.where(ok, _dot_nt(q2, kl) * scale, MASK_VALUE)
        m = jnp.maximum(jnp.maximum(jnp.max(s_loc, axis=-1, keepdims=True), jnp.max(s_ctx, axis=-1, keepdims=True)), sink)
        p_loc = jnp.exp(s_loc - m)
        p_ctx = jnp.exp(s_ctx - m)
        l = jnp.sum(p_loc, axis=-1, keepdims=True) + jnp.sum(p_ctx, axis=-1, keepdims=True) + jnp.exp(sink - m)
        finish((_dot(p_loc.astype(bf16), vl) + _dot(p_ctx.astype(bf16), vc)) / l)


def _window_attention(qkv, sink, batch, t_per_batch):
    m = qkv.shape[0]
    tb = WA_BLOCK
    nb = (t_per_batch - CTX_LEN) // tb
    assert nb * tb >= tb + 2 * WA_WINDOW
    nblk = t_per_batch // tb
    g = WA_HEADS // WA_KV_HEADS
    qcol = ROPE_LO // (g * HEAD_DIM)
    kcol = (ROPE_LO + WA_WIDTH) // HEAD_DIM
    vcol = kcol + WA_KV_HEADS
    return pl.pallas_call(
        functools.partial(_wa_kernel, nb=nb),
        grid=(batch, WA_KV_HEADS, nblk),
        in_specs=[
            pl.BlockSpec(memory_space=pltpu.SMEM),
            pl.BlockSpec((tb, g * HEAD_DIM), lambda b, kh, j: (b * nblk + j, qcol + kh)),
            pl.BlockSpec((t_per_batch, HEAD_DIM), lambda b, kh, j: (b, kcol + kh)),
            pl.BlockSpec((t_per_batch, HEAD_DIM), lambda b, kh, j: (b, vcol + kh)),
        ],
        out_specs=pl.BlockSpec((tb, g * HEAD_DIM), lambda b, kh, j: (b * nblk + j, kh)),
        out_shape=jax.ShapeDtypeStruct((m, WA_WIDTH), bf16),
        compiler_params=_cparams(("arbitrary", "arbitrary", "arbitrary")),
        name="window_attention",
    )(sink.astype(f32), qkv, qkv, qkv)


def _conv_kernel(xp_ref, xc_ref, xn_ref, w_ref, b_ref, xs_ref, bc_ref, *, nblk):
    j = pl.program_id(0) % nblk
    tm = xc_ref.shape[0]
    prev_ok = jnp.where(j >= 2, 1.0, 0.0)
    next_ok = jnp.where((j >= 1) & (j <= nblk - 2), 1.0, 0.0)
    ext = jnp.concatenate([xp_ref[...] * prev_ok, xc_ref[...], xn_ref[...] * next_ok], axis=0)
    n_ext = tm + 2 * CONV_HALO
    acc = jnp.zeros((tm, SSD_XBC_WIDTH), f32) + b_ref[...]
    for k in range(SSD_CONV):
        shifted = pltpu.roll(ext, (SSD_CONV // 2 - k) % n_ext, 0)[CONV_HALO:CONV_HALO + tm]
        acc = acc + shifted * w_ref[k:k + 1, :]
    y = _silu(acc)
    xs_ref[...] = y[:, 0:SSD_D_INNER]
    bc_ref[...] = y[:, SSD_D_INNER:].astype(bf16)


def _ssd_conv(xbc, conv_w, conv_b, t_per_batch):
    m = xbc.shape[0]
    tm = ROW_BLOCK
    nblk = t_per_batch // tm
    hb = tm // CONV_HALO
    last = m // CONV_HALO - 1
    return pl.pallas_call(
        functools.partial(_conv_kernel, nblk=nblk),
        grid=(m // tm,),
        in_specs=[
            pl.BlockSpec((CONV_HALO, SSD_XBC_WIDTH), lambda i: (jnp.maximum(i * hb - 1, 0), 0)),
            pl.BlockSpec((tm, SSD_XBC_WIDTH), lambda i: (i, 0)),
            pl.BlockSpec((CONV_HALO, SSD_XBC_WIDTH), lambda i: (jnp.minimum((i + 1) * hb, last), 0)),
            pl.BlockSpec((SSD_CONV, SSD_XBC_WIDTH), lambda i: (0, 0)),
            pl.BlockSpec((1, SSD_XBC_WIDTH), lambda i: (0, 0)),
        ],
        out_specs=[
            pl.BlockSpec((tm, SSD_D_INNER), lambda i: (i, 0)),
            pl.BlockSpec((tm, SSD_BC_WIDTH), lambda i: (i, 0)),
        ],
        out_shape=[
            jax.ShapeDtypeStruct((m, SSD_D_INNER), f32),
            jax.ShapeDtypeStruct((m, SSD_BC_WIDTH), bf16),
        ],
        compiler_params=_cparams(("arbitrary",)),
        name="ssd_conv",
    )(xbc, xbc, xbc, conv_w, conv_b.reshape(1, -1))


def _expand_heads(small, e_ref):
    hi = small.astype(bf16)
    lo = (small - hi.astype(f32)).astype(bf16)
    e = e_ref[...]
    return _dot(hi, e) + _dot(lo, e)


def _ssd_kernel(xs_ref, bc_ref, dt_ref, dtT_ref, alog_ref, alogc_ref, dtb_ref, dtbc_ref, dsk_ref, e_ref,
                y_ref, state_ref):
    dirn = pl.program_id(1)
    j = pl.program_id(2)
    lc = SSD_CHUNK
    nh = SSD_HEADS
    gw = SSD_GROUP_WIDTH
    fwd = dirn == 0

    @pl.when(j == 0)
    def _():
        state_ref[...] = jnp.zeros_like(state_ref)

    dt_all = _softplus(dt_ref[...] + dtb_ref[...])
    dt = jnp.where(fwd, dt_all[:, 0:nh], dt_all[:, nh:2 * nh])
    dtT_all = _softplus(dtT_ref[...] + dtbc_ref[...])
    dtT = jnp.where(fwd, dtT_all[0:nh], dtT_all[nh:2 * nh])
    a_row = -jnp.exp(jnp.where(fwd, alog_ref[0:1, :], alog_ref[1:2, :]))
    a_col = -jnp.exp(jnp.where(fwd, alogc_ref[0], alogc_ref[1]))
    d_row = jnp.where(fwd, dsk_ref[0:1, :], dsk_ref[1:2, :])
    da = dt * a_row
    daT = dtT * a_col
    ri = lax.broadcasted_iota(i32, (lc, lc), 0)
    ci = lax.broadcasted_iota(i32, (lc, lc), 1)
    tri = jnp.where(fwd, ri - ci, ci - ri) >= 0
    trif = tri.astype(f32)
    acs = jnp.dot(trif, da, preferred_element_type=f32, precision=lax.Precision.HIGHEST)
    acsT = lax.dot_general(daT, trif, (((1,), (1,)), ((), ())), preferred_element_type=f32,
                           precision=lax.Precision.HIGHEST)
    tot = jnp.sum(da, axis=0, keepdims=True)
    small = jnp.concatenate([jnp.exp(acs), jnp.exp(tot - acs) * dt,
                             jnp.broadcast_to(jnp.exp(tot), (8, nh)), jnp.broadcast_to(d_row, (8, nh))], axis=0)
    big = _expand_heads(small, e_ref)
    eacs_x = big[0:lc]
    w_x = big[lc:2 * lc]
    dec_x = big[2 * lc:2 * lc + 1]
    dsk_x = big[2 * lc + 8:2 * lc + 9]

    x = xs_ref[...]
    xw = (x * w_x).astype(bf16)
    xb = x.astype(bf16)
    for g in range(SSD_GROUPS):
        lo = g * gw
        bg = bc_ref[:, g * SSD_D_STATE:(g + 1) * SSD_D_STATE]
        cg = bc_ref[:, (SSD_GROUPS + g) * SSD_D_STATE:(SSD_GROUPS + g + 1) * SSD_D_STATE]
        cb = _dot_nt(cg, bg)
        st = state_ref[g]
        y_inter = _dot(cg, st.astype(bf16))
        state_ref[g] = st * dec_x[:, lo:lo + gw] + _dot_tn(bg, xw[:, lo:lo + gw])
        ys = []
        for k in range(SSD_HEADS_PER_GROUP):
            h = g * SSD_HEADS_PER_GROUP + k
            seg = jnp.exp(jnp.where(tri, acs[:, h:h + 1] - acsT[h:h + 1, :], -jnp.inf))
            mat = (cb * seg * dtT[h:h + 1, :]).astype(bf16)
            ys.append(_dot(mat, xb[:, h * SSD_HEAD_DIM:(h + 1) * SSD_HEAD_DIM]))
        y_g = jnp.concatenate(ys, axis=-1) + y_inter * eacs_x[:, lo:lo + gw] + x[:, lo:lo + gw] * dsk_x[:, lo:lo + gw]
        y_ref[0, :, lo:lo + gw] = y_g


def _ssd_scan(xs, bc, dt_raw, dtT_raw, a_log, dt_bias, d_skip, batch, t_per_batch):
    m = xs.shape[0]
    lc = SSD_CHUNK
    nch = t_per_batch // lc
    nctx = CTX_LEN // lc

    def chunk(b, d, j):
        rev = jnp.where(j < nctx, nctx - 1 - j, nch - 1 + nctx - j)
        return b * nch + jnp.where(d == 0, j, rev)

    expand = (jnp.arange(SSD_D_INNER)[None, :] // SSD_HEAD_DIM == jnp.arange(SSD_HEADS)[:, None]).astype(bf16)
    const2 = lambda b, d, j: (0, 0)
    return pl.pallas_call(
        _ssd_kernel,
        grid=(batch, 2, nch),
        in_specs=[
            pl.BlockSpec((lc, SSD_D_INNER), lambda b, d, j: (chunk(b, d, j), 0)),
            pl.BlockSpec((lc, SSD_BC_WIDTH), lambda b, d, j: (chunk(b, d, j), 0)),
            pl.BlockSpec((lc, DT_WIDTH), lambda b, d, j: (chunk(b, d, j), 0)),
            pl.BlockSpec((DT_WIDTH, lc), lambda b, d, j: (0, chunk(b, d, j))),
            pl.BlockSpec((2, SSD_HEADS), const2),
            pl.BlockSpec((2, SSD_HEADS, 1), lambda b, d, j: (0, 0, 0)),
            pl.BlockSpec((1, DT_WIDTH), const2),
            pl.BlockSpec((DT_WIDTH, 1), const2),
            pl.BlockSpec((2, SSD_HEADS), const2),
            pl.BlockSpec((SSD_HEADS, SSD_D_INNER), const2),
        ],
        out_specs=pl.BlockSpec((1, lc, SSD_D_INNER), lambda b, d, j: (d, chunk(b, d, j), 0)),
        out_shape=jax.ShapeDtypeStruct((2, m, SSD_D_INNER), f32),
        scratch_shapes=[pltpu.VMEM((SSD_GROUPS, SSD_D_STATE, SSD_GROUP_WIDTH), f32)],
        compiler_params=_cparams(("arbitrary", "arbitrary", "arbitrary")),
        name="ssd_scan",
    )(xs, bc, dt_raw, dtT_raw, a_log.astype(f32), a_log.astype(f32).reshape(2, SSD_HEADS, 1),
      dt_bias.astype(f32).reshape(1, DT_WIDTH), dt_bias.astype(f32).reshape(DT_WIDTH, 1), d_skip.astype(f32), expand)


def _outproj_kernel(oa_ref, ob_ref, y_ref, z_ref, sgain_ref, w_ref, x_ref, mod_ref, ngain_ref, xmid_ref, hf_ref):
    d = D_MODEL
    u = (y_ref[0] + y_ref[1]) * _silu(z_ref[...])
    parts = []
    for g in range(SSD_GROUPS):
        ug = u[:, g * SSD_GROUP_WIDTH:(g + 1) * SSD_GROUP_WIDTH]
        parts.append(ug * lax.rsqrt(jnp.mean(ug * ug, axis=-1, keepdims=True) + EPS))
    gn = (jnp.concatenate(parts, axis=-1) * sgain_ref[...]).astype(bf16)
    acc = (_dot(oa_ref[...], w_ref[0:NA_WIDTH, :]) + _dot(ob_ref[...], w_ref[NA_WIDTH:NA_WIDTH + WA_WIDTH, :])
           + _dot(gn, w_ref[NA_WIDTH + WA_WIDTH:, :]))
    mod = mod_ref[0]
    xm = x_ref[...] + mod[:, 2 * d:3 * d] * acc
    xmid_ref[...] = xm
    hf_ref[...] = _rms_mod(xm, ngain_ref[...], mod[:, 3 * d:4 * d], mod[:, 4 * d:5 * d])


def _out_projection(o_a, o_b, y2, z, ssd_norm, w_out, xa, mod4, ngain, t_per_batch):
    m, d = xa.shape
    tm = ROW_BLOCK
    nblk = t_per_batch // tm
    row = lambda i: (i, 0)
    const = lambda i: (0, 0)
    mod_row = lambda i: (jnp.where(i % nblk == 0, 2, i // nblk), 0, 0)
    return pl.pallas_call(
        _outproj_kernel,
        grid=(m // tm,),
        in_specs=[
            pl.BlockSpec((tm, NA_WIDTH), row),
            pl.BlockSpec((tm, WA_WIDTH), row),
            pl.BlockSpec((2, tm, SSD_D_INNER), lambda i: (0, i, 0)),
            pl.BlockSpec((tm, SSD_D_INNER), row),
            _resident((1, SSD_D_INNER), const),
            _resident((d, d), const),
            pl.BlockSpec((tm, d), row),
            pl.BlockSpec((1, 1, 6 * d), mod_row),
            _resident((1, d), const),
        ],
        out_specs=[pl.BlockSpec((tm, d), row), pl.BlockSpec((tm, d), row)],
        out_shape=[jax.ShapeDtypeStruct((m, d), f32), jax.ShapeDtypeStruct((m, d), f32)],
        compiler_params=_cparams(("arbitrary",)),
        name="out_projection",
    )(o_a, o_b, y2, z, ssd_norm.reshape(1, -1), w_out.astype(bf16), xa, mod4, ngain.reshape(1, d))


def _router_kernel(hf_ref, wrT_ref, rb_ref, e_ref, gate_ref, rank_ref, cnt_ref, carry_ref):
    i = pl.program_id(0)
    tm = hf_ref.shape[0]
    ng, ge = N_EXPERT_GROUPS, EXPERTS_PER_GROUP

    @pl.when(i == 0)
    def _():
        carry_ref[...] = jnp.zeros_like(carry_ref)

    aff = _sigmoid(_dot_nt(wrT_ref[...], hf_ref[...].astype(bf16)))
    sel3 = (aff + rb_ref[...]).reshape(ng, ge, tm)
    aff3 = aff.reshape(ng, ge, tm)
    io = lax.broadcasted_iota(i32, (ng, ge, tm), 1)
    m1 = jnp.max(sel3, axis=1, keepdims=True)
    i1 = jnp.min(jnp.where(sel3 == m1, io, ge), axis=1, keepdims=True)
    rest = jnp.where(io == i1, -jnp.inf, sel3)
    m2 = jnp.max(rest, axis=1, keepdims=True)
    i2 = jnp.min(jnp.where(rest == m2, io, ge), axis=1, keepdims=True)
    a1 = jnp.sum(jnp.where(io == i1, aff3, 0.0), axis=1)
    a2 = jnp.sum(jnp.where(io == i2, aff3, 0.0), axis=1)
    score = (m1 + m2)[:, 0, :]
    gi = lax.broadcasted_iota(i32, (ng, tm), 0)
    best = jnp.max(score, axis=0, keepdims=True)
    gb = jnp.min(jnp.where(score == best, gi, ng), axis=0, keepdims=True)
    picked = gi == gb
    pick_i = lambda a: jnp.sum(jnp.where(picked, a, 0), axis=0, keepdims=True)
    pick_f = lambda a: jnp.sum(jnp.where(picked, a, 0.0), axis=0, keepdims=True)
    e1 = gb * ge + pick_i(i1[:, 0, :])
    e2 = gb * ge + pick_i(i2[:, 0, :])
    g1 = pick_f(a1)
    g2 = pick_f(a2)
    den = g1 + g2
    e_ref[...] = jnp.concatenate([e1, e2], axis=0)
    gate_ref[...] = jnp.concatenate([g1 / den, g2 / den], axis=0)

    ei = lax.broadcasted_iota(i32, (N_EXPERTS, tm), 0)
    o1 = jnp.where(ei == e1, 1.0, 0.0)
    o2 = jnp.where(ei == e2, 1.0, 0.0)
    both = o1 + o2
    upper = jnp.where(lax.broadcasted_iota(i32, (tm, tm), 0) <= lax.broadcasted_iota(i32, (tm, tm), 1), 1.0, 0.0)
    incl = _dot(both.astype(bf16), upper.astype(bf16))
    before = incl - both + carry_ref[...]
    r1 = jnp.sum(o1 * before, axis=0, keepdims=True)
    r2 = jnp.sum(o2 * before, axis=0, keepdims=True)
    rank_ref[...] = jnp.concatenate([r1, r2], axis=0).astype(i32)
    total = carry_ref[...] + jnp.sum(both, axis=1, keepdims=True)
    carry_ref[...] = total
    cnt_ref[...] = jnp.broadcast_to(total, cnt_ref.shape).astype(i32)


def _router(hf, w_router, router_bias):
    m, d = hf.shape
    tm = ROW_BLOCK
    return pl.pallas_call(
        _router_kernel,
        grid=(m // tm,),
        in_specs=[
            pl.BlockSpec((tm, d), lambda i: (i, 0)),
            pl.BlockSpec((N_EXPERTS, d), lambda i: (0, 0)),
            pl.BlockSpec((N_EXPERTS, 1), lambda i: (0, 0)),
        ],
        out_specs=[
            pl.BlockSpec((2, tm), lambda i: (0, i)),
            pl.BlockSpec((2, tm), lambda i: (0, i)),
            pl.BlockSpec((2, tm), lambda i: (0, i)),
            pl.BlockSpec((N_EXPERTS, 128), lambda i: (0, 0)),
        ],
        out_shape=[
            jax.ShapeDtypeStruct((2, m), i32),
            jax.ShapeDtypeStruct((2, m), f32),
            jax.ShapeDtypeStruct((2, m), i32),
            jax.ShapeDtypeStruct((N_EXPERTS, 128), i32),
        ],
        scratch_shapes=[pltpu.VMEM((N_EXPERTS, 1), f32)],
        compiler_params=_cparams(("arbitrary",)),
        name="router",
    )(hf, w_router.T.astype(bf16), router_bias.astype(f32).reshape(N_EXPERTS, 1))


def _dispatch_kernel(slot_ref, hf_ref, xin_ref, xbuf_ref, sem):
    del xin_ref
    tm = slot_ref.shape[1]

    def row_copy(r, s):
        return pltpu.make_async_copy(hf_ref.at[pl.ds(r, 1)], xbuf_ref.at[pl.ds(s, 1)], sem)

    def issue(q, carry):
        for u in range(ROW_UNROLL):
            r = q * ROW_UNROLL + u
            for k in range(2):
                row_copy(r, slot_ref[k, r]).start()
        return carry

    lax.fori_loop(0, tm // ROW_UNROLL, issue, 0)

    def drain(q, carry):
        for _ in range(2 * ROW_UNROLL):
            row_copy(0, 0).wait()
        return carry

    lax.fori_loop(0, tm // ROW_UNROLL, drain, 0)


def _dispatch(hf, slot, n_pad):
    m, d = hf.shape
    tm = ROW_BLOCK
    xzero = jnp.zeros((n_pad, d), f32)
    return pl.pallas_call(
        _dispatch_kernel,
        grid=(m // tm,),
        in_specs=[
            pl.BlockSpec((2, tm), lambda i: (0, i), memory_space=pltpu.SMEM),
            pl.BlockSpec((tm, d), lambda i: (i, 0)),
            pl.BlockSpec(memory_space=pl.ANY),
        ],
        out_specs=pl.BlockSpec(memory_space=pl.ANY),
        out_shape=jax.ShapeDtypeStruct((n_pad, d), f32),
        scratch_shapes=[pltpu.SemaphoreType.DMA],
        input_output_aliases={2: 0},
        compiler_params=_cparams(("arbitrary",)),
        name="moe_dispatch",
    )(slot, hf, xzero)


def _moe_kernel(be_ref, nb_ref, x_ref, wg_ref, wu_ref, wd_ref, y_ref):
    del be_ref
    used = pl.program_id(0) < nb_ref[0]

    @pl.when(used)
    def _():
        x = x_ref[...].astype(bf16)
        g = _dot(x, wg_ref[0])
        u = _dot(x, wu_ref[0])
        y_ref[...] = _dot((_silu(g) * u).astype(bf16), wd_ref[0])

    @pl.when(jnp.logical_not(used))
    def _():
        y_ref[...] = jnp.zeros_like(y_ref)


def _moe_experts(xbuf, block_expert, n_blocks_used, wg, wu, wd):
    n_pad, d = xbuf.shape
    tme = MOE_BLOCK
    f = wg.shape[2]
    blk = lambda i, be, nb: (jnp.minimum(i, nb[0] - 1), 0)
    wsel = lambda i, be, nb: (be[i], 0, 0)
    return pl.pallas_call(
        _moe_kernel,
        grid_spec=pltpu.PrefetchScalarGridSpec(
            num_scalar_prefetch=2,
            grid=(n_pad // tme,),
            in_specs=[
                pl.BlockSpec((tme, d), blk),
                pl.BlockSpec((1, d, f), wsel),
                pl.BlockSpec((1, d, f), wsel),
                pl.BlockSpec((1, f, d), wsel),
            ],
            out_specs=pl.BlockSpec((tme, d), lambda i, be, nb: (i, 0)),
        ),
        out_shape=jax.ShapeDtypeStruct((n_pad, d), f32),
        compiler_params=_cparams(("arbitrary",)),
        name="moe_experts",
    )(block_expert, n_blocks_used, xbuf, wg, wu, wd)


def _combine_kernel(slot_ref, ybuf_ref, gate_ref, x_ref, mod_ref, fgain_ref, o_ref, buf_ref, sem, *, final_norm):
    tm = x_ref.shape[0]
    d = D_MODEL

    def row_copy(k, r, s):
        return pltpu.make_async_copy(ybuf_ref.at[pl.ds(s, 1)], buf_ref.at[k, pl.ds(r, 1)], sem)

    def issue(q, carry):
        for u in range(ROW_UNROLL):
            r = q * ROW_UNROLL + u
            for k in range(2):
                row_copy(k, r, slot_ref[k, r]).start()
        return carry

    lax.fori_loop(0, tm // ROW_UNROLL, issue, 0)

    def drain(q, carry):
        for _ in range(ROW_UNROLL):
            for k in range(2):
                row_copy(k, 0, 0).wait()
        return carry

    lax.fori_loop(0, tm // ROW_UNROLL, drain, 0)
    gate = gate_ref[...]
    y = buf_ref[0] * gate[:, 0:1] + buf_ref[1] * gate[:, 1:2]
    xn = x_ref[...] + mod_ref[0][:, 5 * d:6 * d] * y
    if final_norm:
        xn = (xn * lax.rsqrt(jnp.mean(xn * xn, axis=-1, keepdims=True) + EPS)) * fgain_ref[...]
    o_ref[...] = xn


def _combine(ybuf, slot, gate_t, xmid, mod4, fgain, batch, t_per_batch, latent_only):
    m, d = xmid.shape
    tm = ROW_BLOCK
    nblk = t_per_batch // tm
    if latent_only:
        grid = (batch, nblk - 1)
        src = lambda b, j: b * nblk + 1 + j
        dst = lambda b, j: b * (nblk - 1) + j
        modr = lambda b, j: b
        m_out = batch * (t_per_batch - CTX_LEN)
    else:
        grid = (batch, nblk)
        src = dst = lambda b, j: b * nblk + j
        modr = lambda b, j: jnp.where(j == 0, 2, b)
        m_out = m
    return pl.pallas_call(
        functools.partial(_combine_kernel, final_norm=latent_only),
        grid=grid,
        in_specs=[
            pl.BlockSpec((2, tm), lambda b, j: (0, src(b, j)), memory_space=pltpu.SMEM),
            pl.BlockSpec(memory_space=pl.ANY),
            pl.BlockSpec((tm, 2), lambda b, j: (src(b, j), 0)),
            pl.BlockSpec((tm, d), lambda b, j: (src(b, j), 0)),
            pl.BlockSpec((1, 1, 6 * d), lambda b, j: (modr(b, j), 0, 0)),
            pl.BlockSpec((1, d), lambda b, j: (0, 0)),
        ],
        out_specs=pl.BlockSpec((tm, d), lambda b, j: (dst(b, j), 0)),
        out_shape=jax.ShapeDtypeStruct((m_out, d), f32),
        scratch_shapes=[pltpu.VMEM((2, tm, d), f32), pltpu.SemaphoreType.DMA],
        compiler_params=_cparams(("arbitrary", "arbitrary")),
        name="moe_combine",
    )(slot, ybuf, gate_t, xmid, mod4, fgain.reshape(1, d))


def _moe_plan(expert, rank, counts, n_assign):
    tme = MOE_BLOCK
    n_blocks = -(-n_assign // tme) + N_EXPERTS
    blocks_per_expert = (counts + tme - 1) // tme
    block_end = jnp.cumsum(blocks_per_expert)
    row_start = (block_end - blocks_per_expert) * tme
    chosen = expert[..., None] == jnp.arange(N_EXPERTS)
    slot = rank + jnp.sum(jnp.where(chosen, row_start, 0), axis=-1)
    block_expert = jnp.minimum(jnp.sum(block_end[None, :] <= jnp.arange(n_blocks)[:, None], axis=1), N_EXPERTS - 1)
    return slot.astype(i32), block_expert.astype(i32), block_end[-1:].astype(i32), n_blocks * tme


def kernel(x, c, ctx, c_ctx, w_ada, b_ada, norm_mix, norm_ffn, norm_final, w_in, w_out, na_rpb, wa_sink,
           ssd_conv_w, ssd_conv_b, ssd_a_log, ssd_dt_bias, ssd_d, ssd_norm, w_router, router_bias,
           w_gate, w_up, w_down):
    batch, seq, d = x.shape
    depth = w_ada.shape[0]
    assert d == D_MODEL and ctx.shape[1] == CTX_LEN and seq % ROW_BLOCK == 0 and batch <= 2
    t_per_batch = CTX_LEN + seq
    m = batch * t_per_batch
    rows = seq // GRID_W

    xa = jnp.concatenate([ctx, x], axis=1).reshape(m, d)
    cc = jnp.zeros((8, d), f32).at[0:batch].set(c).at[2].set(c_ctx)
    mod_all = _ada_modulation(cc, w_ada, b_ada).reshape(depth, 8, 1, 6 * d)
    rope_tabs = _rope_tables(seq)

    out = None
    for l in range(depth):
        mod4 = mod_all[l]
        qkv, z, xbc, dt_raw, dtT_raw = _in_projection(xa, mod4, norm_mix[l], w_in[l], rope_tabs, t_per_batch)
        o_a = _neighbourhood_attention(qkv, _na_bias_tables(na_rpb[l], rows), batch, t_per_batch)
        o_b = _window_attention(qkv, wa_sink[l], batch, t_per_batch)
        xs, bc = _ssd_conv(xbc, ssd_conv_w[l], ssd_conv_b[l], t_per_batch)
        y2 = _ssd_scan(xs, bc, dt_raw, dtT_raw, ssd_a_log[l], ssd_dt_bias[l], ssd_d[l], batch, t_per_batch)
        xmid, hf = _out_projection(o_a, o_b, y2, z, ssd_norm[l], w_out[l], xa, mod4, norm_ffn[l], t_per_batch)
        expert, gate, rank, cnt = _router(hf, w_router, router_bias)
        slot, block_expert, n_used, n_pad = _moe_plan(expert, rank, cnt[:, 0], 2 * m)
        xbuf = _dispatch(hf, slot, n_pad)
        ybuf = _moe_experts(xbuf, block_expert, n_used, w_gate[l].astype(bf16), w_up[l].astype(bf16),
                            w_down[l].astype(bf16))
        last = l == depth - 1
        res = _combine(ybuf, slot, gate.T, xmid, mod4, norm_final, batch, t_per_batch, latent_only=last)
        if last:
            out = res.reshape(batch, seq, d)
        else:
            xa = res
    return out
```

```python
import functools

import jax
import jax.numpy as jnp
import numpy as np
from jax import lax
from jax.experimental import pallas as pl
from jax.experimental.pallas import tpu as pltpu

f32 = jnp.float32
bf16 = jnp.bfloat16
i32 = jnp.int32

D_MODEL = 2048
GRID_W = 64
CTX_LEN = 256
EPS = 1e-6
MASK_VALUE = -1e30
HEAD_DIM = 128
NA_WIDTH = D_MODEL // 4
NA_HEADS = NA_WIDTH // HEAD_DIM
NA_WIN_R = 8
NA_WIN_C = 16
WA_WIDTH = D_MODEL // 4
WA_HEADS = WA_WIDTH // HEAD_DIM
WA_KV_HEADS = 2
WA_KV_WIDTH = WA_KV_HEADS * HEAD_DIM
WA_WINDOW = 128
WA_BLOCK = 128
ROPE_BASE = 10000.0
SSD_D_INNER = D_MODEL // 2
SSD_HEAD_DIM = 64
SSD_HEADS = SSD_D_INNER // SSD_HEAD_DIM
SSD_GROUPS = 2
SSD_D_STATE = 128
SSD_CONV = 5
SSD_BC_WIDTH = 2 * SSD_GROUPS * SSD_D_STATE
SSD_XBC_WIDTH = SSD_D_INNER + SSD_BC_WIDTH
SSD_GROUP_WIDTH = SSD_D_INNER // SSD_GROUPS
SSD_HEADS_PER_GROUP = SSD_HEADS // SSD_GROUPS
N_EXPERTS = 32
N_EXPERT_GROUPS = 4
EXPERTS_PER_GROUP = N_EXPERTS // N_EXPERT_GROUPS
D_FF_EXPERT = D_MODEL // 2

QKV_WIDTH = 3 * NA_WIDTH + WA_WIDTH + 2 * WA_KV_WIDTH
ROPE_LO = 3 * NA_WIDTH
ROPE_HI = ROPE_LO + WA_WIDTH + WA_KV_WIDTH
Z_LO = QKV_WIDTH
XBC_LO = Z_LO + SSD_D_INNER
DT_LO = XBC_LO + SSD_XBC_WIDTH
DT_WIDTH = 2 * SSD_HEADS

ROW_BLOCK = 256
SSD_CHUNK = 128
NA_GROUP_ROWS = 4
NA_KEY_ROWS = NA_GROUP_ROWS + NA_WIN_R
MOE_BLOCK = 256
MOE_CAST_ROWS = 256
ROW_UNROLL = 8
CONV_HALO = 8
VMEM_LIMIT = 56 * 1024 * 1024


def _cparams(sem):
    return pltpu.CompilerParams(dimension_semantics=sem, vmem_limit_bytes=VMEM_LIMIT)


def _sigmoid(x):
    return 1.0 / (1.0 + jnp.exp(-x))


def _silu(x):
    return x * _sigmoid(x)


def _softplus(x):
    return jnp.maximum(x, 0.0) + jnp.log1p(jnp.exp(-jnp.abs(x)))


def _dot(a, b):
    return jnp.dot(a, b, preferred_element_type=f32)


def _dot_nt(a, b):
    return lax.dot_general(a, b, (((1,), (1,)), ((), ())), preferred_element_type=f32)


def _dot_tn(a, b):
    return lax.dot_general(a, b, (((0,), (0,)), ((), ())), preferred_element_type=f32)


def _resident(shape, index_map):
    return pl.BlockSpec(shape, index_map, pipeline_mode=pl.Buffered(1))


def _ada_kernel(c_ref, w_ref, b_ref, o_ref):
    s = _silu(c_ref[...]).astype(bf16)
    o_ref[0] = _dot(s, w_ref[0].astype(bf16)) + b_ref[0]


def _ada_modulation(cc, w_ada, b_ada):
    depth, d, n = w_ada.shape
    tn = 1024
    return pl.pallas_call(
        _ada_kernel,
        grid=(depth, n // tn),
        in_specs=[
            pl.BlockSpec((8, d), lambda l, j: (0, 0)),
            pl.BlockSpec((1, d, tn), lambda l, j: (l, 0, j)),
            pl.BlockSpec((1, 1, tn), lambda l, j: (l, 0, j)),
        ],
        out_specs=pl.BlockSpec((1, 8, tn), lambda l, j: (l, 0, j)),
        out_shape=jax.ShapeDtypeStruct((depth, 8, n), f32),
        compiler_params=_cparams(("arbitrary", "arbitrary")),
        name="ada_modulation",
    )(cc, w_ada, b_ada.reshape(depth, 1, n))


def _rms_mod(x, gain, shift, scale):
    y = x * lax.rsqrt(jnp.mean(x * x, axis=-1, keepdims=True) + EPS)
    return (y * gain) * (1.0 + scale) + shift


def _inproj_kernel(x_ref, mod_ref, gain_ref, wqkv_ref, wz_ref, wxbc_ref, wdt_ref, wdtT_ref,
                   cos_ref, sina_ref, sinb_ref, qkv_ref, z_ref, xbc_ref, dt_ref, dtT_ref):
    d = D_MODEL
    mod = mod_ref[0]
    h = _rms_mod(x_ref[...], gain_ref[...], mod[:, 0:d], mod[:, d:2 * d]).astype(bf16)
    acc = _dot(h, wqkv_ref[...])
    qkv_ref[:, 0:ROPE_LO] = acc[:, 0:ROPE_LO].astype(bf16)
    cos, sina, sinb = cos_ref[...], sina_ref[...], sinb_ref[...]
    quarter = HEAD_DIM // 4
    for c0 in range(ROPE_LO, ROPE_HI, HEAD_DIM):
        xh = acc[:, c0:c0 + HEAD_DIM]
        rot = (xh * cos + pltpu.roll(xh, HEAD_DIM - quarter, 1) * sina + pltpu.roll(xh, quarter, 1) * sinb)
        qkv_ref[:, c0:c0 + HEAD_DIM] = rot.astype(bf16)
    qkv_ref[:, ROPE_HI:QKV_WIDTH] = acc[:, ROPE_HI:QKV_WIDTH].astype(bf16)
    z_ref[...] = _dot(h, wz_ref[...])
    xbc_ref[...] = _dot(h, wxbc_ref[...])
    dt_ref[...] = _dot(h, wdt_ref[...])
    dtT_ref[...] = _dot_nt(wdtT_ref[...], h)


def _in_projection(xa, mod4, gain, w_in, rope_tabs, t_per_batch):
    m, d = xa.shape
    tm = ROW_BLOCK
    nblk = t_per_batch // tm
    wb = w_in.astype(bf16)
    wqkv = wb[:, 0:QKV_WIDTH]
    wz = wb[:, Z_LO:XBC_LO]
    wxbc = wb[:, XBC_LO:DT_LO]
    wdt = wb[:, DT_LO:DT_LO + DT_WIDTH]
    wdtT = wdt.T
    cos, sina, sinb = rope_tabs
    row = lambda i: (i, 0)
    const = lambda i: (0, 0)
    mod_row = lambda i: (jnp.where(i % nblk == 0, 2, i // nblk), 0, 0)
    tab_row = lambda i: (i % nblk, 0)
    return pl.pallas_call(
        _inproj_kernel,
        grid=(m // tm,),
        in_specs=[
            pl.BlockSpec((tm, d), row),
            pl.BlockSpec((1, 1, 6 * d), mod_row),
            _resident((1, d), const),
            _resident((d, QKV_WIDTH), const),
            _resident((d, SSD_D_INNER), const),
            _resident((d, SSD_XBC_WIDTH), const),
            _resident((d, DT_WIDTH), const),
            _resident((DT_WIDTH, d), const),
            pl.BlockSpec((tm, HEAD_DIM), tab_row),
            pl.BlockSpec((tm, HEAD_DIM), tab_row),
            pl.BlockSpec((tm, HEAD_DIM), tab_row),
        ],
        out_specs=[
            pl.BlockSpec((tm, QKV_WIDTH), row),
            pl.BlockSpec((tm, SSD_D_INNER), row),
            pl.BlockSpec((tm, SSD_XBC_WIDTH), row),
            pl.BlockSpec((tm, DT_WIDTH), row),
            pl.BlockSpec((DT_WIDTH, tm), lambda i: (0, i)),
        ],
        out_shape=[
            jax.ShapeDtypeStruct((m, QKV_WIDTH), bf16),
            jax.ShapeDtypeStruct((m, SSD_D_INNER), f32),
            jax.ShapeDtypeStruct((m, SSD_XBC_WIDTH), f32),
            jax.ShapeDtypeStruct((m, DT_WIDTH), f32),
            jax.ShapeDtypeStruct((DT_WIDTH, m), f32),
        ],
        compiler_params=_cparams(("arbitrary",)),
        name="in_projection",
    )(xa, mod4, gain.reshape(1, d), wqkv, wz, wxbc, wdt, wdtT, cos, sina, sinb)


def _rope_tables(seq):
    t = jnp.arange(seq, dtype=i32)
    q = HEAD_DIM // 4
    inv = jnp.power(ROPE_BASE, -jnp.arange(q, dtype=f32) / q)
    ang_r = (t // GRID_W).astype(f32)[:, None] * inv
    ang_c = (t % GRID_W).astype(f32)[:, None] * inv
    zero = jnp.zeros_like(ang_r)
    cos = jnp.concatenate([jnp.cos(ang_r), jnp.cos(ang_r), jnp.cos(ang_c), jnp.cos(ang_c)], axis=-1)
    sina = jnp.concatenate([-jnp.sin(ang_r), zero, -jnp.sin(ang_c), zero], axis=-1)
    sinb = jnp.concatenate([zero, jnp.sin(ang_r), zero, jnp.sin(ang_c)], axis=-1)
    pad = lambda a, v: jnp.concatenate([jnp.full((CTX_LEN, HEAD_DIM), v, f32), a], axis=0)
    return pad(cos, 1.0), pad(sina, 0.0), pad(sinb, 0.0)


def _na_bias_tables(rpb, rows):
    col = np.arange(GRID_W)
    c0 = np.clip(col - NA_WIN_C // 2, 0, GRID_W - NA_WIN_C)
    col_ok = (col[None, :] >= c0[:, None]) & (col[None, :] < c0[:, None] + NA_WIN_C)
    dc = np.clip(col[None, :] - col[:, None] + NA_WIN_C - 1, 0, 2 * NA_WIN_C - 2)
    dc_hot = (dc[:, :, None] == np.arange(2 * NA_WIN_C - 1)).astype(np.float32)
    exact = lax.Precision.HIGHEST
    by_col = jnp.einsum("hde,qke->hdqk", rpb.astype(f32), dc_hot, precision=exact)
    tabs = []
    for r_first in (NA_GROUP_ROWS, 0, rows - NA_GROUP_ROWS):
        start = min(max(r_first - NA_WIN_R // 2, 0), rows - NA_KEY_ROWS)
        r = r_first + np.arange(NA_GROUP_ROWS)
        kr = start + np.arange(NA_KEY_ROWS)
        r0 = np.clip(r - NA_WIN_R // 2, 0, rows - NA_WIN_R)
        row_ok = (kr[None, :] >= r0[:, None]) & (kr[None, :] < r0[:, None] + NA_WIN_R)
        dr = np.clip(kr[None, :] - r[:, None] + NA_WIN_R - 1, 0, 2 * NA_WIN_R - 2)
        dr_hot = (dr[:, :, None] == np.arange(2 * NA_WIN_R - 1)).astype(np.float32)
        b = jnp.einsum("gid,hdqk->hgqik", dr_hot, by_col, precision=exact)
        ok = row_ok[:, None, :, None] & col_ok[None, :, None, :]
        b = jnp.where(ok[None], b, MASK_VALUE)
        tabs.append(b.reshape(rpb.shape[0], NA_GROUP_ROWS * GRID_W, NA_KEY_ROWS * GRID_W))
    return jnp.stack(tabs, axis=1)


def _na_kernel(q_ref, k_ref, v_ref, bias_ref, o_ref, *, rows):
    g = pl.program_id(1)
    scale = HEAD_DIM ** -0.5
    nk = NA_KEY_ROWS * GRID_W

    def head_cols(h):
        return slice(h * HEAD_DIM, (h + 1) * HEAD_DIM)

    @pl.when(g == 0)
    def _():
        for h in range(NA_HEADS):
            hc = head_cols(h)
            s_ctx = _dot_nt(q_ref[:, hc], k_ref[0:CTX_LEN, hc]) * scale
            m = jnp.max(s_ctx, axis=-1, keepdims=True)
            p = jnp.exp(s_ctx - m)
            l = jnp.sum(p, axis=-1, keepdims=True)
            o_ref[:, hc] = (_dot(p.astype(bf16), v_ref[0:CTX_LEN, hc]) / l).astype(bf16)

    @pl.when(g > 0)
    def _():
        r_first = (g - 1) * NA_GROUP_ROWS
        start = jnp.clip(r_first - NA_WIN_R // 2, 0, rows - NA_KEY_ROWS)
        off = pl.multiple_of(CTX_LEN + start * GRID_W, GRID_W)
        for h in range(NA_HEADS):
            hc = head_cols(h)
            q = q_ref[:, hc]
            vc = v_ref[0:CTX_LEN, hc]
            s_ctx = _dot_nt(q, k_ref[0:CTX_LEN, hc]) * scale
            s_loc = _dot_nt(q, k_ref[pl.ds(off, nk), hc]) * scale + bias_ref[h, 0]
            m = jnp.maximum(jnp.max(s_loc, axis=-1, keepdims=True), jnp.max(s_ctx, axis=-1, keepdims=True))
            p_loc = jnp.exp(s_loc - m)
            p_ctx = jnp.exp(s_ctx - m)
            l = jnp.sum(p_loc, axis=-1, keepdims=True) + jnp.sum(p_ctx, axis=-1, keepdims=True)
            o = _dot(p_loc.astype(bf16), v_ref[pl.ds(off, nk), hc]) + _dot(p_ctx.astype(bf16), vc)
            o_ref[:, hc] = (o / l).astype(bf16)


def _neighbourhood_attention(qkv, bias_tabs, batch, t_per_batch):
    m = qkv.shape[0]
    rows = (t_per_batch - CTX_LEN) // GRID_W
    tq = NA_GROUP_ROWS * GRID_W
    assert tq == CTX_LEN and rows % NA_GROUP_ROWS == 0 and rows >= NA_KEY_ROWS + NA_GROUP_ROWS
    ng = rows // NA_GROUP_ROWS
    nblk = t_per_batch // tq
    bias_idx = lambda b, g: (0, jnp.where(g == 1, 1, jnp.where(g == ng, 2, 0)), 0, 0)
    kv_spec = lambda col: pl.BlockSpec((t_per_batch, NA_WIDTH), lambda b, g: (b, col), pipeline_mode=pl.Buffered(1))
    return pl.pallas_call(
        functools.partial(_na_kernel, rows=rows),
        grid=(batch, ng + 1),
        in_specs=[
            pl.BlockSpec((tq, NA_WIDTH), lambda b, g: (b * nblk + g, 0)),
            kv_spec(1),
            kv_spec(2),
            pl.BlockSpec((NA_HEADS, 1, tq, NA_KEY_ROWS * GRID_W), bias_idx),
        ],
        out_specs=pl.BlockSpec((tq, NA_WIDTH), lambda b, g: (b * nblk + g, 0)),
        out_shape=jax.ShapeDtypeStruct((m, NA_WIDTH), bf16),
        compiler_params=_cparams(("arbitrary", "arbitrary")),
        name="neighbourhood_attention",
    )(qkv, qkv, qkv, bias_tabs)


def _wa_kernel(sink_ref, q_ref, k_ref, v_ref, o_ref, *, nb):
    j = pl.program_id(1)
    scale = HEAD_DIM ** -0.5
    tb = WA_BLOCK
    g = WA_HEADS // WA_KV_HEADS
    nk = tb + 2 * WA_WINDOW
    rowi = lax.broadcasted_iota(i32, (g * tb, 1), 0)

    def stacked_q(kh):
        return jnp.concatenate([q_ref[:, (kh * g + i) * HEAD_DIM:(kh * g + i + 1) * HEAD_DIM] for i in range(g)], axis=0)

    def sink_col(kh):
        sink = jnp.full((g * tb, 1), sink_ref[kh * g], f32)
        for i in range(1, g):
            sink = jnp.where(rowi >= i * tb, sink_ref[kh * g + i], sink)
        return sink

    def finish(kh, o):
        for i in range(g):
            o_ref[:, (kh * g + i) * HEAD_DIM:(kh * g + i + 1) * HEAD_DIM] = o[i * tb:(i + 1) * tb].astype(bf16)

    @pl.when(j < CTX_LEN // tb)
    def _():
        for kh in range(WA_KV_HEADS):
            kcols = slice(kh * HEAD_DIM, (kh + 1) * HEAD_DIM)
            sink = sink_col(kh)
            s_ctx = _dot_nt(stacked_q(kh), k_ref[0:CTX_LEN, kcols]) * scale
            m = jnp.maximum(jnp.max(s_ctx, axis=-1, keepdims=True), sink)
            p = jnp.exp(s_ctx - m)
            l = jnp.sum(p, axis=-1, keepdims=True) + jnp.exp(sink - m)
            finish(kh, _dot(p.astype(bf16), v_ref[0:CTX_LEN, kcols]) / l)

    @pl.when(j >= CTX_LEN // tb)
    def _():
        n = j - CTX_LEN // tb
        st = jnp.clip(n * tb - WA_WINDOW, 0, nb * tb - nk)
        off = pl.multiple_of(CTX_LEN + st, tb)
        qpos = n * tb + lax.broadcasted_iota(i32, (g * tb, nk), 0) % tb
        kpos = st + lax.broadcasted_iota(i32, (g * tb, nk), 1)
        ok = jnp.abs(qpos - kpos) <= WA_WINDOW
        for kh in range(WA_KV_HEADS):
            kcols = slice(kh * HEAD_DIM, (kh + 1) * HEAD_DIM)
            sink = sink_col(kh)
            q2 = stacked_q(kh)
            s_ctx = _dot_nt(q2, k_ref[0:CTX_LEN, kcols]) * scale
            s_loc = jnp.where(ok, _dot_nt(q2, k_ref[pl.ds(off, nk), kcols]) * scale, MASK_VALUE)
            m = jnp.maximum(jnp.maximum(jnp.max(s_loc, axis=-1, keepdims=True), jnp.max(s_ctx, axis=-1, keepdims=True)), sink)
            p_loc = jnp.exp(s_loc - m)
            p_ctx = jnp.exp(s_ctx - m)
            l = jnp.sum(p_loc, axis=-1, keepdims=True) + jnp.sum(p_ctx, axis=-1, keepdims=True) + jnp.exp(sink - m)
            o = _dot(p_loc.astype(bf16), v_ref[pl.ds(off, nk), kcols]) + _dot(p_ctx.astype(bf16), v_ref[0:CTX_LEN, kcols])
            finish(kh, o / l)


def _window_attention(qkv, sink, batch, t_per_batch):
    m = qkv.shape[0]
    tb = WA_BLOCK
    nb = (t_per_batch - CTX_LEN) // tb
    assert nb * tb >= tb + 2 * WA_WINDOW
    nblk = t_per_batch // tb
    qcol = ROPE_LO // WA_WIDTH
    kcol = (ROPE_LO + WA_WIDTH) // WA_KV_WIDTH
    kv_spec = lambda col: pl.BlockSpec((t_per_batch, WA_KV_WIDTH), lambda b, j: (b, col), pipeline_mode=pl.Buffered(1))
    return pl.pallas_call(
        functools.partial(_wa_kernel, nb=nb),
        grid=(batch, nblk),
        in_specs=[
            pl.BlockSpec(memory_space=pltpu.SMEM),
            pl.BlockSpec((tb, WA_WIDTH), lambda b, j: (b * nblk + j, qcol)),
            kv_spec(kcol),
            kv_spec(kcol + 1),
        ],
        out_specs=pl.BlockSpec((tb, WA_WIDTH), lambda b, j: (b * nblk + j, 0)),
        out_shape=jax.ShapeDtypeStruct((m, WA_WIDTH), bf16),
        compiler_params=_cparams(("arbitrary", "arbitrary")),
        name="window_attention",
    )(sink.astype(f32), qkv, qkv, qkv)


def _conv_kernel(xp_ref, xc_ref, xn_ref, w_ref, b_ref, xs_ref, bc_ref, *, nblk):
    j = pl.program_id(0) % nblk
    tm = xc_ref.shape[0]
    prev_ok = jnp.where(j >= 2, 1.0, 0.0)
    next_ok = jnp.where((j >= 1) & (j <= nblk - 2), 1.0, 0.0)
    ext = jnp.concatenate([xp_ref[...] * prev_ok, xc_ref[...], xn_ref[...] * next_ok], axis=0)
    n_ext = tm + 2 * CONV_HALO
    acc = jnp.zeros((tm, SSD_XBC_WIDTH), f32) + b_ref[...]
    for k in range(SSD_CONV):
        shifted = pltpu.roll(ext, (SSD_CONV // 2 - k) % n_ext, 0)[CONV_HALO:CONV_HALO + tm]
        acc = acc + shifted * w_ref[k:k + 1, :]
    y = _silu(acc)
    xs_ref[...] = y[:, 0:SSD_D_INNER]
    bc_ref[...] = y[:, SSD_D_INNER:].astype(bf16)


def _ssd_conv(xbc, conv_w, conv_b, t_per_batch):
    m = xbc.shape[0]
    tm = ROW_BLOCK
    nblk = t_per_batch // tm
    hb = tm // CONV_HALO
    last = m // CONV_HALO - 1
    return pl.pallas_call(
        functools.partial(_conv_kernel, nblk=nblk),
        grid=(m // tm,),
        in_specs=[
            pl.BlockSpec((CONV_HALO, SSD_XBC_WIDTH), lambda i: (jnp.maximum(i * hb - 1, 0), 0)),
            pl.BlockSpec((tm, SSD_XBC_WIDTH), lambda i: (i, 0)),
            pl.BlockSpec((CONV_HALO, SSD_XBC_WIDTH), lambda i: (jnp.minimum((i + 1) * hb, last), 0)),
            pl.BlockSpec((SSD_CONV, SSD_XBC_WIDTH), lambda i: (0, 0)),
            pl.BlockSpec((1, SSD_XBC_WIDTH), lambda i: (0, 0)),
        ],
        out_specs=[
            pl.BlockSpec((tm, SSD_D_INNER), lambda i: (i, 0)),
            pl.BlockSpec((tm, SSD_BC_WIDTH), lambda i: (i, 0)),
        ],
        out_shape=[
            jax.ShapeDtypeStruct((m, SSD_D_INNER), f32),
            jax.ShapeDtypeStruct((m, SSD_BC_WIDTH), bf16),
        ],
        compiler_params=_cparams(("arbitrary",)),
        name="ssd_conv",
    )(xbc, xbc, xbc, conv_w, conv_b.reshape(1, -1))


def _expand_heads(small, e_ref):
    hi = small.astype(bf16)
    lo = (small - hi.astype(f32)).astype(bf16)
    e = e_ref[...]
    return _dot(hi, e) + _dot(lo, e)


def _ssd_direction(fwd, xs_ref, bc_ref, dt_ref, dtT_ref, alog_ref, alogc_ref, dtb_ref, dtbc_ref, dsk_ref, e_ref,
                   y_ref, state_ref):
    di = 0 if fwd else 1
    lc = SSD_CHUNK
    nh = SSD_HEADS
    gw = SSD_GROUP_WIDTH
    hs = slice(di * nh, (di + 1) * nh)
    dt = _softplus(dt_ref[:, hs] + dtb_ref[:, hs])
    dtT = _softplus(dtT_ref[hs, :] + dtbc_ref[hs, :])
    a_row = -jnp.exp(alog_ref[di:di + 1, :])
    a_col = -jnp.exp(alogc_ref[di])
    d_row = dsk_ref[di:di + 1, :]
    da = dt * a_row
    daT = dtT * a_col
    ri = lax.broadcasted_iota(i32, (lc, lc), 0)
    ci = lax.broadcasted_iota(i32, (lc, lc), 1)
    tri = (ri >= ci) if fwd else (ri <= ci)
    trif = tri.astype(f32)
    acs = jnp.dot(trif, da, preferred_element_type=f32, precision=lax.Precision.HIGHEST)
    acsT = lax.dot_general(daT, trif, (((1,), (1,)), ((), ())), preferred_element_type=f32,
                           precision=lax.Precision.HIGHEST)
    tot = jnp.sum(da, axis=0, keepdims=True)
    small = jnp.concatenate([jnp.exp(acs), jnp.exp(tot - acs) * dt,
                             jnp.broadcast_to(jnp.exp(tot), (8, nh)), jnp.broadcast_to(d_row, (8, nh))], axis=0)
    big = _expand_heads(small, e_ref)
    eacs_x = big[0:lc]
    w_x = big[lc:2 * lc]
    dec_x = big[2 * lc:2 * lc + 1]
    dsk_x = big[2 * lc + 8:2 * lc + 9]

    x = xs_ref[...]
    xw = (x * w_x).astype(bf16)
    xb = x.astype(bf16)
    for g in range(SSD_GROUPS):
        lo = g * gw
        bg = bc_ref[:, g * SSD_D_STATE:(g + 1) * SSD_D_STATE]
        cg = bc_ref[:, (SSD_GROUPS + g) * SSD_D_STATE:(SSD_GROUPS + g + 1) * SSD_D_STATE]
        cb = _dot_nt(cg, bg)
        st = state_ref[di, g]
        y_inter = _dot(cg, st.astype(bf16))
        state_ref[di, g] = st * dec_x[:, lo:lo + gw] + _dot_tn(bg, xw[:, lo:lo + gw])
        ys = []
        for k in range(SSD_HEADS_PER_GROUP):
            h = g * SSD_HEADS_PER_GROUP + k
            seg = jnp.exp(jnp.where(tri, acs[:, h:h + 1] - acsT[h:h + 1, :], -jnp.inf))
            mat = (cb * seg * dtT[h:h + 1, :]).astype(bf16)
            ys.append(_dot(mat, xb[:, h * SSD_HEAD_DIM:(h + 1) * SSD_HEAD_DIM]))
        y_g = jnp.concatenate(ys, axis=-1) + y_inter * eacs_x[:, lo:lo + gw] + x[:, lo:lo + gw] * dsk_x[:, lo:lo + gw]
        y_ref[:, lo:lo + gw] = y_g


def _ssd_kernel(xsf_ref, bcf_ref, dtf_ref, dtTf_ref, xsb_ref, bcb_ref, dtb_ref, dtTb_ref,
                alog_ref, alogc_ref, bias_ref, biasc_ref, dsk_ref, e_ref, yf_ref, yb_ref, state_ref):
    @pl.when(pl.program_id(1) == 0)
    def _():
        state_ref[...] = jnp.zeros_like(state_ref)

    params = (alog_ref, alogc_ref, bias_ref, biasc_ref, dsk_ref, e_ref)
    _ssd_direction(True, xsf_ref, bcf_ref, dtf_ref, dtTf_ref, *params, yf_ref, state_ref)
    _ssd_direction(False, xsb_ref, bcb_ref, dtb_ref, dtTb_ref, *params, yb_ref, state_ref)


def _ssd_scan(xs, bc, dt_raw, dtT_raw, a_log, dt_bias, d_skip, batch, t_per_batch):
    m = xs.shape[0]
    lc = SSD_CHUNK
    nch = t_per_batch // lc
    nctx = CTX_LEN // lc

    def fwd_chunk(b, j):
        return b * nch + j

    def bwd_chunk(b, j):
        return b * nch + jnp.where(j < nctx, nctx - 1 - j, nch - 1 + nctx - j)

    def streams(chunk):
        return [
            pl.BlockSpec((lc, SSD_D_INNER), lambda b, j: (chunk(b, j), 0)),
            pl.BlockSpec((lc, SSD_BC_WIDTH), lambda b, j: (chunk(b, j), 0)),
            pl.BlockSpec((lc, DT_WIDTH), lambda b, j: (chunk(b, j), 0)),
            pl.BlockSpec((DT_WIDTH, lc), lambda b, j: (0, chunk(b, j))),
        ]

    expand = (jnp.arange(SSD_D_INNER)[None, :] // SSD_HEAD_DIM == jnp.arange(SSD_HEADS)[:, None]).astype(bf16)
    const2 = lambda b, j: (0, 0)
    return pl.pallas_call(
        _ssd_kernel,
        grid=(batch, nch),
        in_specs=streams(fwd_chunk) + streams(bwd_chunk) + [
            pl.BlockSpec((2, SSD_HEADS), const2),
            pl.BlockSpec((2, SSD_HEADS, 1), lambda b, j: (0, 0, 0)),
            pl.BlockSpec((1, DT_WIDTH), const2),
            pl.BlockSpec((DT_WIDTH, 1), const2),
            pl.BlockSpec((2, SSD_HEADS), const2),
            pl.BlockSpec((SSD_HEADS, SSD_D_INNER), const2),
        ],
        out_specs=[
            pl.BlockSpec((lc, SSD_D_INNER), lambda b, j: (fwd_chunk(b, j), 0)),
            pl.BlockSpec((lc, SSD_D_INNER), lambda b, j: (bwd_chunk(b, j), 0)),
        ],
        out_shape=[jax.ShapeDtypeStruct((m, SSD_D_INNER), f32), jax.ShapeDtypeStruct((m, SSD_D_INNER), f32)],
        scratch_shapes=[pltpu.VMEM((2, SSD_GROUPS, SSD_D_STATE, SSD_GROUP_WIDTH), f32)],
        compiler_params=_cparams(("arbitrary", "arbitrary")),
        name="ssd_scan",
    )(xs, bc, dt_raw, dtT_raw, xs, bc, dt_raw, dtT_raw,
      a_log.astype(f32), a_log.astype(f32).reshape(2, SSD_HEADS, 1),
      dt_bias.astype(f32).reshape(1, DT_WIDTH), dt_bias.astype(f32).reshape(DT_WIDTH, 1), d_skip.astype(f32), expand)


def _outproj_kernel(oa_ref, ob_ref, yf_ref, yb_ref, z_ref, sgain_ref, w_ref, x_ref, mod_ref, ngain_ref, xmid_ref, hf_ref):
    d = D_MODEL
    u = (yf_ref[...] + yb_ref[...]) * _silu(z_ref[...])
    parts = []
    for g in range(SSD_GROUPS):
        ug = u[:, g * SSD_GROUP_WIDTH:(g + 1) * SSD_GROUP_WIDTH]
        parts.append(ug * lax.rsqrt(jnp.mean(ug * ug, axis=-1, keepdims=True) + EPS))
    gn = (jnp.concatenate(parts, axis=-1) * sgain_ref[...]).astype(bf16)
    acc = (_dot(oa_ref[...], w_ref[0:NA_WIDTH, :]) + _dot(ob_ref[...], w_ref[NA_WIDTH:NA_WIDTH + WA_WIDTH, :])
           + _dot(gn, w_ref[NA_WIDTH + WA_WIDTH:, :]))
    mod = mod_ref[0]
    xm = x_ref[...] + mod[:, 2 * d:3 * d] * acc
    xmid_ref[...] = xm
    hf_ref[...] = _rms_mod(xm, ngain_ref[...], mod[:, 3 * d:4 * d], mod[:, 4 * d:5 * d])


def _out_projection(o_a, o_b, y_f, y_b, z, ssd_norm, w_out, xa, mod4, ngain, t_per_batch):
    m, d = xa.shape
    tm = ROW_BLOCK
    nblk = t_per_batch // tm
    row = lambda i: (i, 0)
    const = lambda i: (0, 0)
    mod_row = lambda i: (jnp.where(i % nblk == 0, 2, i // nblk), 0, 0)
    return pl.pallas_call(
        _outproj_kernel,
        grid=(m // tm,),
        in_specs=[
            pl.BlockSpec((tm, NA_WIDTH), row),
            pl.BlockSpec((tm, WA_WIDTH), row),
            pl.BlockSpec((tm, SSD_D_INNER), row),
            pl.BlockSpec((tm, SSD_D_INNER), row),
            pl.BlockSpec((tm, SSD_D_INNER), row),
            _resident((1, SSD_D_INNER), const),
            _resident((d, d), const),
            pl.BlockSpec((tm, d), row),
            pl.BlockSpec((1, 1, 6 * d), mod_row),
            _resident((1, d), const),
        ],
        out_specs=[pl.BlockSpec((tm, d), row), pl.BlockSpec((tm, d), row)],
        out_shape=[jax.ShapeDtypeStruct((m, d), f32), jax.ShapeDtypeStruct((m, d), f32)],
        compiler_params=_cparams(("arbitrary",)),
        name="out_projection",
    )(o_a, o_b, y_f, y_b, z, ssd_norm.reshape(1, -1), w_out.astype(bf16), xa, mod4, ngain.reshape(1, d))


def _router_kernel(hf_ref, wrT_ref, rb_ref, e_ref, gate_ref, rank_ref, cnt_ref, carry_ref):
    i = pl.program_id(0)
    tm = hf_ref.shape[0]
    ng, ge = N_EXPERT_GROUPS, EXPERTS_PER_GROUP

    @pl.when(i == 0)
    def _():
        carry_ref[...] = jnp.zeros_like(carry_ref)

    aff = _sigmoid(_dot_nt(wrT_ref[...], hf_ref[...].astype(bf16)))
    sel3 = (aff + rb_ref[...]).reshape(ng, ge, tm)
    aff3 = aff.reshape(ng, ge, tm)
    io = lax.broadcasted_iota(i32, (ng, ge, tm), 1)
    m1 = jnp.max(sel3, axis=1, keepdims=True)
    i1 = jnp.min(jnp.where(sel3 == m1, io, ge), axis=1, keepdims=True)
    rest = jnp.where(io == i1, -jnp.inf, sel3)
    m2 = jnp.max(rest, axis=1, keepdims=True)
    i2 = jnp.min(jnp.where(rest == m2, io, ge), axis=1, keepdims=True)
    a1 = jnp.sum(jnp.where(io == i1, aff3, 0.0), axis=1)
    a2 = jnp.sum(jnp.where(io == i2, aff3, 0.0), axis=1)
    score = (m1 + m2)[:, 0, :]
    gi = lax.broadcasted_iota(i32, (ng, tm), 0)
    best = jnp.max(score, axis=0, keepdims=True)
    gb = jnp.min(jnp.where(score == best, gi, ng), axis=0, keepdims=True)
    picked = gi == gb
    pick_i = lambda a: jnp.sum(jnp.where(picked, a, 0), axis=0, keepdims=True)
    pick_f = lambda a: jnp.sum(jnp.where(picked, a, 0.0), axis=0, keepdims=True)
    e1 = gb * ge + pick_i(i1[:, 0, :])
    e2 = gb * ge + pick_i(i2[:, 0, :])
    g1 = pick_f(a1)
    g2 = pick_f(a2)
    den = g1 + g2
    e_ref[...] = jnp.concatenate([e1, e2], axis=0)
    gate_ref[...] = jnp.concatenate([g1 / den, g2 / den], axis=0)

    ei = lax.broadcasted_iota(i32, (N_EXPERTS, tm), 0)
    o1 = jnp.where(ei == e1, 1.0, 0.0)
    o2 = jnp.where(ei == e2, 1.0, 0.0)
    both = o1 + o2
    upper = jnp.where(lax.broadcasted_iota(i32, (tm, tm), 0) <= lax.broadcasted_iota(i32, (tm, tm), 1), 1.0, 0.0)
    incl = _dot(both.astype(bf16), upper.astype(bf16))
    before = incl - both + carry_ref[...]
    r1 = jnp.sum(o1 * before, axis=0, keepdims=True)
    r2 = jnp.sum(o2 * before, axis=0, keepdims=True)
    rank_ref[...] = jnp.concatenate([r1, r2], axis=0).astype(i32)
    total = carry_ref[...] + jnp.sum(both, axis=1, keepdims=True)
    carry_ref[...] = total
    cnt_ref[...] = jnp.broadcast_to(total, cnt_ref.shape).astype(i32)


def _router(hf, w_router, router_bias):
    m, d = hf.shape
    tm = ROW_BLOCK
    return pl.pallas_call(
        _router_kernel,
        grid=(m // tm,),
        in_specs=[
            pl.BlockSpec((tm, d), lambda i: (i, 0)),
            pl.BlockSpec((N_EXPERTS, d), lambda i: (0, 0)),
            pl.BlockSpec((N_EXPERTS, 1), lambda i: (0, 0)),
        ],
        out_specs=[
            pl.BlockSpec((2, tm), lambda i: (0, i)),
            pl.BlockSpec((2, tm), lambda i: (0, i)),
            pl.BlockSpec((2, tm), lambda i: (0, i)),
            pl.BlockSpec((N_EXPERTS, 128), lambda i: (0, 0)),
        ],
        out_shape=[
            jax.ShapeDtypeStruct((2, m), i32),
            jax.ShapeDtypeStruct((2, m), f32),
            jax.ShapeDtypeStruct((2, m), i32),
            jax.ShapeDtypeStruct((N_EXPERTS, 128), i32),
        ],
        scratch_shapes=[pltpu.VMEM((N_EXPERTS, 1), f32)],
        compiler_params=_cparams(("arbitrary",)),
        name="router",
    )(hf, w_router.T.astype(bf16), router_bias.astype(f32).reshape(N_EXPERTS, 1))


def _dispatch_kernel(last_block_ref, tail_ref, slot_ref, hf_ref, xbuf_ref, zero_ref, sem, zero_sem):
    tm = slot_ref.shape[1]

    @pl.when(pl.program_id(0) == 0)
    def _():
        zero_ref[...] = jnp.zeros_like(zero_ref)

        def block_copy(b):
            r0 = pl.multiple_of(b * MOE_BLOCK, MOE_BLOCK)
            return pltpu.make_async_copy(zero_ref, xbuf_ref.at[pl.ds(r0, MOE_BLOCK)], zero_sem)

        def per_expert(fn):
            def body(e, carry):
                @pl.when(last_block_ref[e] >= 0)
                def _():
                    fn(last_block_ref[e])
                return carry
            lax.fori_loop(0, N_EXPERTS, body, 0)

        def per_tail(fn):
            def body(b, carry):
                fn(b)
                return carry
            lax.fori_loop(tail_ref[0], tail_ref[1], body, 0)

        per_expert(lambda b: block_copy(b).start())
        per_tail(lambda b: block_copy(b).start())
        per_expert(lambda b: block_copy(0).wait())
        per_tail(lambda b: block_copy(0).wait())

    def row_copy(r, s):
        return pltpu.make_async_copy(hf_ref.at[pl.ds(r, 1)], xbuf_ref.at[pl.ds(s, 1)], sem)

    def issue(q, carry):
        for u in range(ROW_UNROLL):
            r = q * ROW_UNROLL + u
            for k in range(2):
                row_copy(r, slot_ref[k, r]).start()
        return carry

    lax.fori_loop(0, tm // ROW_UNROLL, issue, 0)

    def drain(q, carry):
        for _ in range(2 * ROW_UNROLL):
            row_copy(0, 0).wait()
        return carry

    lax.fori_loop(0, tm // ROW_UNROLL, drain, 0)


def _dispatch(hf, slot, last_block, tail, n_pad):
    m, d = hf.shape
    tm = ROW_BLOCK
    return pl.pallas_call(
        _dispatch_kernel,
        grid_spec=pltpu.PrefetchScalarGridSpec(
            num_scalar_prefetch=2,
            grid=(m // tm,),
            in_specs=[
                pl.BlockSpec((2, tm), lambda i, lb, tl: (0, i), memory_space=pltpu.SMEM),
                pl.BlockSpec((tm, d), lambda i, lb, tl: (i, 0)),
            ],
            out_specs=pl.BlockSpec(memory_space=pl.ANY),
            scratch_shapes=[pltpu.VMEM((MOE_BLOCK, d), f32), pltpu.SemaphoreType.DMA, pltpu.SemaphoreType.DMA],
        ),
        out_shape=jax.ShapeDtypeStruct((n_pad, d), f32),
        compiler_params=_cparams(("arbitrary",)),
        name="moe_dispatch",
    )(last_block, tail, slot, hf)


def _block_state(be_ref, nb_ref):
    i = pl.program_id(0)
    used = i < nb_ref[0]
    fresh = jnp.logical_or(i == 0, be_ref[i] != be_ref[jnp.maximum(i - 1, 0)])
    return used, jnp.logical_and(used, fresh)


def _cast_weight(src_ref, dst_ref):
    k = dst_ref.shape[0]
    ck = MOE_CAST_ROWS

    def chunk(c, carry):
        r0 = pl.multiple_of(c * ck, ck)
        dst_ref[pl.ds(r0, ck), :] = src_ref[0, pl.ds(r0, ck), :].astype(bf16)
        return carry

    lax.fori_loop(0, k // ck, chunk, 0)


def _moe_up_kernel(be_ref, nb_ref, x_ref, wg_ref, wu_ref, h_ref, wgb_ref, wub_ref):
    used, fresh = _block_state(be_ref, nb_ref)

    @pl.when(fresh)
    def _():
        _cast_weight(wg_ref, wgb_ref)
        _cast_weight(wu_ref, wub_ref)

    @pl.when(used)
    def _():
        x = x_ref[...].astype(bf16)
        g = _dot(x, wgb_ref[...])
        u = _dot(x, wub_ref[...])
        h_ref[...] = (_silu(g) * u).astype(bf16)

    @pl.when(jnp.logical_not(used))
    def _():
        h_ref[...] = jnp.zeros_like(h_ref)


def _moe_down_kernel(be_ref, nb_ref, h_ref, wd_ref, y_ref, wdb_ref):
    used, fresh = _block_state(be_ref, nb_ref)

    @pl.when(fresh)
    def _():
        _cast_weight(wd_ref, wdb_ref)

    @pl.when(used)
    def _():
        y_ref[...] = _dot(h_ref[...], wdb_ref[...])

    @pl.when(jnp.logical_not(used))
    def _():
        y_ref[...] = jnp.zeros_like(y_ref)


def _moe_experts(xbuf, block_expert, n_blocks_used, wg, wu, wd, layer):
    n_pad, d = xbuf.shape
    tme = MOE_BLOCK
    f = wg.shape[-1]
    blk = lambda i, be, nb: (jnp.minimum(i, nb[0] - 1), 0)
    own = lambda i, be, nb: (i, 0)
    wsel = lambda i, be, nb: (layer, be[i], 0, 0)
    h = pl.pallas_call(
        _moe_up_kernel,
        grid_spec=pltpu.PrefetchScalarGridSpec(
            num_scalar_prefetch=2,
            grid=(n_pad // tme,),
            in_specs=[
                pl.BlockSpec((tme, d), blk),
                pl.BlockSpec((None, 1, d, f), wsel),
                pl.BlockSpec((None, 1, d, f), wsel),
            ],
            out_specs=pl.BlockSpec((tme, f), own),
            scratch_shapes=[pltpu.VMEM((d, f), bf16), pltpu.VMEM((d, f), bf16)],
        ),
        out_shape=jax.ShapeDtypeStruct((n_pad, f), bf16),
        compiler_params=_cparams(("arbitrary",)),
        name="moe_up",
    )(block_expert, n_blocks_used, xbuf, wg, wu)
    return pl.pallas_call(
        _moe_down_kernel,
        grid_spec=pltpu.PrefetchScalarGridSpec(
            num_scalar_prefetch=2,
            grid=(n_pad // tme,),
            in_specs=[
                pl.BlockSpec((tme, f), blk),
                pl.BlockSpec((None, 1, f, d), wsel),
            ],
            out_specs=pl.BlockSpec((tme, d), own),
            scratch_shapes=[pltpu.VMEM((f, d), bf16)],
        ),
        out_shape=jax.ShapeDtypeStruct((n_pad, d), f32),
        compiler_params=_cparams(("arbitrary",)),
        name="moe_down",
    )(block_expert, n_blocks_used, h, wd)


def _combine_kernel(slot_ref, slot_next_ref, ybuf_ref, gate_ref, x_ref, mod_ref, fgain_ref, o_ref, buf_ref, sems, *,
                    final_norm, n_steps):
    s = pl.program_id(0)
    tm = x_ref.shape[0]
    d = D_MODEL
    cur = s % 2

    def row_copy(b, k, r, src):
        return pltpu.make_async_copy(ybuf_ref.at[pl.ds(src, 1)], buf_ref.at[b, k, pl.ds(r, 1)], sems.at[b])

    def gather(rows_ref, b):
        def issue(q, carry):
            for u in range(ROW_UNROLL):
                r = q * ROW_UNROLL + u
                for k in range(2):
                    row_copy(b, k, r, rows_ref[k, r]).start()
            return carry

        lax.fori_loop(0, tm // ROW_UNROLL, issue, 0)

    @pl.when(s == 0)
    def _():
        gather(slot_ref, 0)

    @pl.when(s + 1 < n_steps)
    def _():
        gather(slot_next_ref, 1 - cur)

    def drain(q, carry):
        for _ in range(ROW_UNROLL):
            for k in range(2):
                row_copy(cur, k, 0, 0).wait()
        return carry

    lax.fori_loop(0, tm // ROW_UNROLL, drain, 0)
    gate = gate_ref[...]
    y = buf_ref[cur, 0] * gate[:, 0:1] + buf_ref[cur, 1] * gate[:, 1:2]
    xn = x_ref[...] + mod_ref[0][:, 5 * d:6 * d] * y
    if final_norm:
        xn = (xn * lax.rsqrt(jnp.mean(xn * xn, axis=-1, keepdims=True) + EPS)) * fgain_ref[...]
    o_ref[...] = xn


def _combine(ybuf, slot, gate_t, xmid, mod4, fgain, batch, t_per_batch, latent_only):
    m, d = xmid.shape
    tm = ROW_BLOCK
    nblk = t_per_batch // tm
    if latent_only:
        per_batch = nblk - 1
        src = lambda s: (s // per_batch) * nblk + 1 + s % per_batch
        modr = lambda s: s // per_batch
    else:
        per_batch = nblk
        src = lambda s: s
        modr = lambda s: jnp.where(s % nblk == 0, 2, s // nblk)
    n_steps = batch * per_batch
    nxt = lambda s: src(jnp.minimum(s + 1, n_steps - 1))
    return pl.pallas_call(
        functools.partial(_combine_kernel, final_norm=latent_only, n_steps=n_steps),
        grid=(n_steps,),
        in_specs=[
            pl.BlockSpec((2, tm), lambda s: (0, src(s)), memory_space=pltpu.SMEM),
            pl.BlockSpec((2, tm), lambda s: (0, nxt(s)), memory_space=pltpu.SMEM),
            pl.BlockSpec(memory_space=pl.ANY),
            pl.BlockSpec((tm, 2), lambda s: (src(s), 0)),
            pl.BlockSpec((tm, d), lambda s: (src(s), 0)),
            pl.BlockSpec((1, 1, 6 * d), lambda s: (modr(s), 0, 0)),
            pl.BlockSpec((1, d), lambda s: (0, 0)),
        ],
        out_specs=pl.BlockSpec((tm, d), lambda s: (s, 0)),
        out_shape=jax.ShapeDtypeStruct((n_steps * tm, d), f32),
        scratch_shapes=[pltpu.VMEM((2, 2, tm, d), f32), pltpu.SemaphoreType.DMA((2,))],
        compiler_params=_cparams(("arbitrary",)),
        name="moe_combine",
    )(slot, slot, ybuf, gate_t, xmid, mod4, fgain.reshape(1, d))


def _moe_plan(expert, rank, counts, n_assign):
    tme = MOE_BLOCK
    n_blocks = -(-n_assign // tme) + N_EXPERTS
    blocks_per_expert = (counts + tme - 1) // tme
    block_end = jnp.cumsum(blocks_per_expert)
    row_start = (block_end - blocks_per_expert) * tme
    chosen = expert[..., None] == jnp.arange(N_EXPERTS)
    slot = rank + jnp.sum(jnp.where(chosen, row_start, 0), axis=-1)
    block_expert = jnp.minimum(jnp.sum(block_end[None, :] <= jnp.arange(n_blocks)[:, None], axis=1), N_EXPERTS - 1)
    last_block = jnp.where(blocks_per_expert > 0, block_end - 1, -1)
    tail = jnp.stack([block_end[-1], jnp.asarray(n_blocks, block_end.dtype)])
    return (slot.astype(i32), block_expert.astype(i32), block_end[-1:].astype(i32), last_block.astype(i32),
            tail.astype(i32), n_blocks * tme)


def kernel(x, c, ctx, c_ctx, w_ada, b_ada, norm_mix, norm_ffn, norm_final, w_in, w_out, na_rpb, wa_sink,
           ssd_conv_w, ssd_conv_b, ssd_a_log, ssd_dt_bias, ssd_d, ssd_norm, w_router, router_bias,
           w_gate, w_up, w_down):
    batch, seq, d = x.shape
    depth = w_ada.shape[0]
    assert d == D_MODEL and ctx.shape[1] == CTX_LEN and seq % ROW_BLOCK == 0 and batch <= 2
    t_per_batch = CTX_LEN + seq
    m = batch * t_per_batch
    rows = seq // GRID_W

    xa = jnp.concatenate([ctx, x], axis=1).reshape(m, d)
    cc = jnp.zeros((8, d), f32).at[0:batch].set(c).at[2].set(c_ctx)
    mod_all = _ada_modulation(cc, w_ada, b_ada).reshape(depth, 8, 1, 6 * d)
    rope_tabs = _rope_tables(seq)

    out = None
    for l in range(depth):
        mod4 = mod_all[l]
        qkv, z, xbc, dt_raw, dtT_raw = _in_projection(xa, mod4, norm_mix[l], w_in[l], rope_tabs, t_per_batch)
        o_a = _neighbourhood_attention(qkv, _na_bias_tables(na_rpb[l], rows), batch, t_per_batch)
        o_b = _window_attention(qkv, wa_sink[l], batch, t_per_batch)
        xs, bc = _ssd_conv(xbc, ssd_conv_w[l], ssd_conv_b[l], t_per_batch)
        y_f, y_b = _ssd_scan(xs, bc, dt_raw, dtT_raw, ssd_a_log[l], ssd_dt_bias[l], ssd_d[l], batch, t_per_batch)
        xmid, hf = _out_projection(o_a, o_b, y_f, y_b, z, ssd_norm[l], w_out[l], xa, mod4, norm_ffn[l], t_per_batch)
        expert, gate, rank, cnt = _router(hf, w_router, router_bias)
        slot, block_expert, n_used, last_block, tail, n_pad = _moe_plan(expert, rank, cnt[:, 0], 2 * m)
        xbuf = _dispatch(hf, slot, last_block, tail, n_pad)
        ybuf = _moe_experts(xbuf, block_expert, n_used, w_gate, w_up, w_down, l)
        last = l == depth - 1
        res = _combine(ybuf, slot, gate.T, xmid, mod4, norm_final, batch, t_per_batch, latent_only=last)
        if last:
            out = res.reshape(batch, seq, d)
        else:
            xa = res
    return out
```

```python
import functools

import jax
import jax.numpy as jnp
import numpy as np
from jax import lax
from jax.experimental import pallas as pl
from jax.experimental.pallas import tpu as pltpu

f32 = jnp.float32
bf16 = jnp.bfloat16
i32 = jnp.int32

D_MODEL = 2048
GRID_W = 64
CTX_LEN = 256
EPS = 1e-6
MASK_VALUE = -1e30
HEAD_DIM = 128
NA_WIDTH = D_MODEL // 4
NA_HEADS = NA_WIDTH // HEAD_DIM
NA_WIN_R = 8
NA_WIN_C = 16
WA_WIDTH = D_MODEL // 4
WA_HEADS = WA_WIDTH // HEAD_DIM
WA_KV_HEADS = 2
WA_KV_WIDTH = WA_KV_HEADS * HEAD_DIM
WA_WINDOW = 128
WA_BLOCK = 128
ROPE_BASE = 10000.0
SSD_D_INNER = D_MODEL // 2
SSD_HEAD_DIM = 64
SSD_HEADS = SSD_D_INNER // SSD_HEAD_DIM
SSD_GROUPS = 2
SSD_D_STATE = 128
SSD_CONV = 5
SSD_BC_WIDTH = 2 * SSD_GROUPS * SSD_D_STATE
SSD_XBC_WIDTH = SSD_D_INNER + SSD_BC_WIDTH
SSD_GROUP_WIDTH = SSD_D_INNER // SSD_GROUPS
SSD_HEADS_PER_GROUP = SSD_HEADS // SSD_GROUPS
N_EXPERTS = 32
N_EXPERT_GROUPS = 4
EXPERTS_PER_GROUP = N_EXPERTS // N_EXPERT_GROUPS
D_FF_EXPERT = D_MODEL // 2

QKV_WIDTH = 3 * NA_WIDTH + WA_WIDTH + 2 * WA_KV_WIDTH
ROPE_LO = 3 * NA_WIDTH
ROPE_HI = ROPE_LO + WA_WIDTH + WA_KV_WIDTH
Z_LO = QKV_WIDTH
XBC_LO = Z_LO + SSD_D_INNER
DT_LO = XBC_LO + SSD_XBC_WIDTH
DT_WIDTH = 2 * SSD_HEADS

ROW_BLOCK = 256
SSD_CHUNK = 128
NA_GROUP_ROWS = 4
NA_KEY_ROWS = NA_GROUP_ROWS + NA_WIN_R
MOE_BLOCK = 256
MOE_CAST_ROWS = 256
ROW_UNROLL = 8
CONV_HALO = 8
VMEM_LIMIT = 56 * 1024 * 1024


def _cparams(sem):
    return pltpu.CompilerParams(dimension_semantics=sem, vmem_limit_bytes=VMEM_LIMIT)


def _sigmoid(x):
    return 1.0 / (1.0 + jnp.exp(-x))


def _silu(x):
    return x * _sigmoid(x)


def _softplus(x):
    return jnp.maximum(x, 0.0) + jnp.log1p(jnp.exp(-jnp.abs(x)))


def _dot(a, b):
    return jnp.dot(a, b, preferred_element_type=f32)


def _dot_nt(a, b):
    return lax.dot_general(a, b, (((1,), (1,)), ((), ())), preferred_element_type=f32)


def _dot_tn(a, b):
    return lax.dot_general(a, b, (((0,), (0,)), ((), ())), preferred_element_type=f32)


def _resident(shape, index_map):
    return pl.BlockSpec(shape, index_map, pipeline_mode=pl.Buffered(1))


def _ada_kernel(c_ref, w_ref, b_ref, o_ref):
    s = _silu(c_ref[...]).astype(bf16)
    o_ref[0] = _dot(s, w_ref[0].astype(bf16)) + b_ref[0]


def _ada_modulation(cc, w_ada, b_ada):
    depth, d, n = w_ada.shape
    tn = 1024
    return pl.pallas_call(
        _ada_kernel,
        grid=(depth, n // tn),
        in_specs=[
            pl.BlockSpec((8, d), lambda l, j: (0, 0)),
            pl.BlockSpec((1, d, tn), lambda l, j: (l, 0, j)),
            pl.BlockSpec((1, 1, tn), lambda l, j: (l, 0, j)),
        ],
        out_specs=pl.BlockSpec((1, 8, tn), lambda l, j: (l, 0, j)),
        out_shape=jax.ShapeDtypeStruct((depth, 8, n), f32),
        compiler_params=_cparams(("arbitrary", "arbitrary")),
        name="ada_modulation",
    )(cc, w_ada, b_ada.reshape(depth, 1, n))


def _rms_mod(x, gain, shift, scale):
    y = x * lax.rsqrt(jnp.mean(x * x, axis=-1, keepdims=True) + EPS)
    return (y * gain) * (1.0 + scale) + shift


def _inproj_kernel(x_ref, mod_ref, gain_ref, wqkv_ref, wz_ref, wxbc_ref, wdt_ref, wdtT_ref,
                   cos_ref, sina_ref, sinb_ref, qkv_ref, z_ref, xbc_ref, dt_ref, dtT_ref):
    d = D_MODEL
    mod = mod_ref[0]
    h = _rms_mod(x_ref[...], gain_ref[...], mod[:, 0:d], mod[:, d:2 * d]).astype(bf16)
    acc = _dot(h, wqkv_ref[...])
    qkv_ref[:, 0:ROPE_LO] = acc[:, 0:ROPE_LO].astype(bf16)
    cos, sina, sinb = cos_ref[...], sina_ref[...], sinb_ref[...]
    quarter = HEAD_DIM // 4
    for c0 in range(ROPE_LO, ROPE_HI, HEAD_DIM):
        xh = acc[:, c0:c0 + HEAD_DIM]
        rot = (xh * cos + pltpu.roll(xh, HEAD_DIM - quarter, 1) * sina + pltpu.roll(xh, quarter, 1) * sinb)
        qkv_ref[:, c0:c0 + HEAD_DIM] = rot.astype(bf16)
    qkv_ref[:, ROPE_HI:QKV_WIDTH] = acc[:, ROPE_HI:QKV_WIDTH].astype(bf16)
    z_ref[...] = _dot(h, wz_ref[...]).astype(z_ref.dtype)
    xbc_ref[...] = _dot(h, wxbc_ref[...])
    dt_ref[...] = _dot(h, wdt_ref[...])
    dtT_ref[...] = _dot_nt(wdtT_ref[...], h)


def _in_projection(xa, mod4, gain, w_in, rope_tabs, t_per_batch):
    m, d = xa.shape
    tm = ROW_BLOCK
    nblk = t_per_batch // tm
    wb = w_in.astype(bf16)
    wqkv = wb[:, 0:QKV_WIDTH]
    wz = wb[:, Z_LO:XBC_LO]
    wxbc = wb[:, XBC_LO:DT_LO]
    wdt = wb[:, DT_LO:DT_LO + DT_WIDTH]
    wdtT = wdt.T
    cos, sina, sinb = rope_tabs
    row = lambda i: (i, 0)
    const = lambda i: (0, 0)
    mod_row = lambda i: (jnp.where(i % nblk == 0, 2, i // nblk), 0, 0)
    tab_row = lambda i: (i % nblk, 0)
    return pl.pallas_call(
        _inproj_kernel,
        grid=(m // tm,),
        in_specs=[
            pl.BlockSpec((tm, d), row),
            pl.BlockSpec((1, 1, 6 * d), mod_row),
            _resident((1, d), const),
            _resident((d, QKV_WIDTH), const),
            _resident((d, SSD_D_INNER), const),
            _resident((d, SSD_XBC_WIDTH), const),
            _resident((d, DT_WIDTH), const),
            _resident((DT_WIDTH, d), const),
            pl.BlockSpec((tm, HEAD_DIM), tab_row),
            pl.BlockSpec((tm, HEAD_DIM), tab_row),
            pl.BlockSpec((tm, HEAD_DIM), tab_row),
        ],
        out_specs=[
            pl.BlockSpec((tm, QKV_WIDTH), row),
            pl.BlockSpec((tm, SSD_D_INNER), row),
            pl.BlockSpec((tm, SSD_XBC_WIDTH), row),
            pl.BlockSpec((tm, DT_WIDTH), row),
            pl.BlockSpec((DT_WIDTH, tm), lambda i: (0, i)),
        ],
        out_shape=[
            jax.ShapeDtypeStruct((m, QKV_WIDTH), bf16),
            jax.ShapeDtypeStruct((m, SSD_D_INNER), bf16),
            jax.ShapeDtypeStruct((m, SSD_XBC_WIDTH), f32),
            jax.ShapeDtypeStruct((m, DT_WIDTH), f32),
            jax.ShapeDtypeStruct((DT_WIDTH, m), f32),
        ],
        compiler_params=_cparams(("arbitrary",)),
        name="in_projection",
    )(xa, mod4, gain.reshape(1, d), wqkv, wz, wxbc, wdt, wdtT, cos, sina, sinb)


def _rope_tables(seq):
    t = jnp.arange(seq, dtype=i32)
    q = HEAD_DIM // 4
    inv = jnp.power(ROPE_BASE, -jnp.arange(q, dtype=f32) / q)
    ang_r = (t // GRID_W).astype(f32)[:, None] * inv
    ang_c = (t % GRID_W).astype(f32)[:, None] * inv
    zero = jnp.zeros_like(ang_r)
    cos = jnp.concatenate([jnp.cos(ang_r), jnp.cos(ang_r), jnp.cos(ang_c), jnp.cos(ang_c)], axis=-1)
    sina = jnp.concatenate([-jnp.sin(ang_r), zero, -jnp.sin(ang_c), zero], axis=-1)
    sinb = jnp.concatenate([zero, jnp.sin(ang_r), zero, jnp.sin(ang_c)], axis=-1)
    pad = lambda a, v: jnp.concatenate([jnp.full((CTX_LEN, HEAD_DIM), v, f32), a], axis=0)
    return pad(cos, 1.0), pad(sina, 0.0), pad(sinb, 0.0)


def _na_bias_tables(rpb, rows):
    col = np.arange(GRID_W)
    c0 = np.clip(col - NA_WIN_C // 2, 0, GRID_W - NA_WIN_C)
    col_ok = (col[None, :] >= c0[:, None]) & (col[None, :] < c0[:, None] + NA_WIN_C)
    dc = np.clip(col[None, :] - col[:, None] + NA_WIN_C - 1, 0, 2 * NA_WIN_C - 2)
    dc_hot = (dc[:, :, None] == np.arange(2 * NA_WIN_C - 1)).astype(np.float32)
    exact = lax.Precision.HIGHEST
    by_col = jnp.einsum("hde,qke->hdqk", rpb.astype(f32), dc_hot, precision=exact)
    tabs = []
    for r_first in (NA_GROUP_ROWS, 0, rows - NA_GROUP_ROWS):
        start = min(max(r_first - NA_WIN_R // 2, 0), rows - NA_KEY_ROWS)
        r = r_first + np.arange(NA_GROUP_ROWS)
        kr = start + np.arange(NA_KEY_ROWS)
        r0 = np.clip(r - NA_WIN_R // 2, 0, rows - NA_WIN_R)
        row_ok = (kr[None, :] >= r0[:, None]) & (kr[None, :] < r0[:, None] + NA_WIN_R)
        dr = np.clip(kr[None, :] - r[:, None] + NA_WIN_R - 1, 0, 2 * NA_WIN_R - 2)
        dr_hot = (dr[:, :, None] == np.arange(2 * NA_WIN_R - 1)).astype(np.float32)
        b = jnp.einsum("gid,hdqk->hgqik", dr_hot, by_col, precision=exact)
        ok = row_ok[:, None, :, None] & col_ok[None, :, None, :]
        b = jnp.where(ok[None], b, MASK_VALUE)
        tabs.append(b.reshape(rpb.shape[0], NA_GROUP_ROWS * GRID_W, NA_KEY_ROWS * GRID_W))
    return jnp.stack(tabs, axis=1)


def _na_kernel(q_ref, k_ref, v_ref, bias_ref, o_ref, *, rows):
    g = pl.program_id(1)
    scale = HEAD_DIM ** -0.5
    nk = NA_KEY_ROWS * GRID_W

    def head_cols(h):
        return slice(h * HEAD_DIM, (h + 1) * HEAD_DIM)

    @pl.when(g == 0)
    def _():
        for h in range(NA_HEADS):
            hc = head_cols(h)
            s_ctx = _dot_nt(q_ref[:, hc], k_ref[0:CTX_LEN, hc]) * scale
            m = jnp.max(s_ctx, axis=-1, keepdims=True)
            p = jnp.exp(s_ctx - m)
            l = jnp.sum(p, axis=-1, keepdims=True)
            o_ref[:, hc] = (_dot(p.astype(bf16), v_ref[0:CTX_LEN, hc]) / l).astype(bf16)

    @pl.when(g > 0)
    def _():
        r_first = (g - 1) * NA_GROUP_ROWS
        start = jnp.clip(r_first - NA_WIN_R // 2, 0, rows - NA_KEY_ROWS)
        off = pl.multiple_of(CTX_LEN + start * GRID_W, GRID_W)
        for h in range(NA_HEADS):
            hc = head_cols(h)
            q = q_ref[:, hc]
            vc = v_ref[0:CTX_LEN, hc]
            s_ctx = _dot_nt(q, k_ref[0:CTX_LEN, hc]) * scale
            s_loc = _dot_nt(q, k_ref[pl.ds(off, nk), hc]) * scale + bias_ref[h, 0]
            m = jnp.maximum(jnp.max(s_loc, axis=-1, keepdims=True), jnp.max(s_ctx, axis=-1, keepdims=True))
            p_loc = jnp.exp(s_loc - m)
            p_ctx = jnp.exp(s_ctx - m)
            l = jnp.sum(p_loc, axis=-1, keepdims=True) + jnp.sum(p_ctx, axis=-1, keepdims=True)
            o = _dot(p_loc.astype(bf16), v_ref[pl.ds(off, nk), hc]) + _dot(p_ctx.astype(bf16), vc)
            o_ref[:, hc] = (o / l).astype(bf16)


def _neighbourhood_attention(qkv, bias_tabs, batch, t_per_batch):
    m = qkv.shape[0]
    rows = (t_per_batch - CTX_LEN) // GRID_W
    tq = NA_GROUP_ROWS * GRID_W
    assert tq == CTX_LEN and rows % NA_GROUP_ROWS == 0 and rows >= NA_KEY_ROWS + NA_GROUP_ROWS
    ng = rows // NA_GROUP_ROWS
    nblk = t_per_batch // tq
    bias_idx = lambda b, g: (0, jnp.where(g == 1, 1, jnp.where(g == ng, 2, 0)), 0, 0)
    kv_spec = lambda col: pl.BlockSpec((t_per_batch, NA_WIDTH), lambda b, g: (b, col), pipeline_mode=pl.Buffered(1))
    return pl.pallas_call(
        functools.partial(_na_kernel, rows=rows),
        grid=(batch, ng + 1),
        in_specs=[
            pl.BlockSpec((tq, NA_WIDTH), lambda b, g: (b * nblk + g, 0)),
            kv_spec(1),
            kv_spec(2),
            pl.BlockSpec((NA_HEADS, 1, tq, NA_KEY_ROWS * GRID_W), bias_idx),
        ],
        out_specs=pl.BlockSpec((tq, NA_WIDTH), lambda b, g: (b * nblk + g, 0)),
        out_shape=jax.ShapeDtypeStruct((m, NA_WIDTH), bf16),
        compiler_params=_cparams(("arbitrary", "arbitrary")),
        name="neighbourhood_attention",
    )(qkv, qkv, qkv, bias_tabs)


def _wa_kernel(sink_ref, q_ref, k_ref, v_ref, o_ref, *, nb):
    j = pl.program_id(1)
    scale = HEAD_DIM ** -0.5
    tb = WA_BLOCK
    g = WA_HEADS // WA_KV_HEADS
    nk = tb + 2 * WA_WINDOW
    rowi = lax.broadcasted_iota(i32, (g * tb, 1), 0)

    def stacked_q(kh):
        return jnp.concatenate([q_ref[:, (kh * g + i) * HEAD_DIM:(kh * g + i + 1) * HEAD_DIM] for i in range(g)], axis=0)

    def sink_col(kh):
        sink = jnp.full((g * tb, 1), sink_ref[kh * g], f32)
        for i in range(1, g):
            sink = jnp.where(rowi >= i * tb, sink_ref[kh * g + i], sink)
        return sink

    def finish(kh, o):
        for i in range(g):
            o_ref[:, (kh * g + i) * HEAD_DIM:(kh * g + i + 1) * HEAD_DIM] = o[i * tb:(i + 1) * tb].astype(bf16)

    @pl.when(j < CTX_LEN // tb)
    def _():
        for kh in range(WA_KV_HEADS):
            kcols = slice(kh * HEAD_DIM, (kh + 1) * HEAD_DIM)
            sink = sink_col(kh)
            s_ctx = _dot_nt(stacked_q(kh), k_ref[0:CTX_LEN, kcols]) * scale
            m = jnp.maximum(jnp.max(s_ctx, axis=-1, keepdims=True), sink)
            p = jnp.exp(s_ctx - m)
            l = jnp.sum(p, axis=-1, keepdims=True) + jnp.exp(sink - m)
            finish(kh, _dot(p.astype(bf16), v_ref[0:CTX_LEN, kcols]) / l)

    @pl.when(j >= CTX_LEN // tb)
    def _():
        n = j - CTX_LEN // tb
        st = jnp.clip(n * tb - WA_WINDOW, 0, nb * tb - nk)
        off = pl.multiple_of(CTX_LEN + st, tb)
        qpos = n * tb + lax.broadcasted_iota(i32, (g * tb, nk), 0) % tb
        kpos = st + lax.broadcasted_iota(i32, (g * tb, nk), 1)
        ok = jnp.abs(qpos - kpos) <= WA_WINDOW
        for kh in range(WA_KV_HEADS):
            kcols = slice(kh * HEAD_DIM, (kh + 1) * HEAD_DIM)
            sink = sink_col(kh)
            q2 = stacked_q(kh)
            s_ctx = _dot_nt(q2, k_ref[0:CTX_LEN, kcols]) * scale
            s_loc = jnp.where(ok, _dot_nt(q2, k_ref[pl.ds(off, nk), kcols]) * scale, MASK_VALUE)
            m = jnp.maximum(jnp.maximum(jnp.max(s_loc, axis=-1, keepdims=True), jnp.max(s_ctx, axis=-1, keepdims=True)), sink)
            p_loc = jnp.exp(s_loc - m)
            p_ctx = jnp.exp(s_ctx - m)
            l = jnp.sum(p_loc, axis=-1, keepdims=True) + jnp.sum(p_ctx, axis=-1, keepdims=True) + jnp.exp(sink - m)
            o = _dot(p_loc.astype(bf16), v_ref[pl.ds(off, nk), kcols]) + _dot(p_ctx.astype(bf16), v_ref[0:CTX_LEN, kcols])
            finish(kh, o / l)


def _window_attention(qkv, sink, batch, t_per_batch):
    m = qkv.shape[0]
    tb = WA_BLOCK
    nb = (t_per_batch - CTX_LEN) // tb
    assert nb * tb >= tb + 2 * WA_WINDOW
    nblk = t_per_batch // tb
    qcol = ROPE_LO // WA_WIDTH
    kcol = (ROPE_LO + WA_WIDTH) // WA_KV_WIDTH
    kv_spec = lambda col: pl.BlockSpec((t_per_batch, WA_KV_WIDTH), lambda b, j: (b, col), pipeline_mode=pl.Buffered(1))
    return pl.pallas_call(
        functools.partial(_wa_kernel, nb=nb),
        grid=(batch, nblk),
        in_specs=[
            pl.BlockSpec(memory_space=pltpu.SMEM),
            pl.BlockSpec((tb, WA_WIDTH), lambda b, j: (b * nblk + j, qcol)),
            kv_spec(kcol),
            kv_spec(kcol + 1),
        ],
        out_specs=pl.BlockSpec((tb, WA_WIDTH), lambda b, j: (b * nblk + j, 0)),
        out_shape=jax.ShapeDtypeStruct((m, WA_WIDTH), bf16),
        compiler_params=_cparams(("arbitrary", "arbitrary")),
        name="window_attention",
    )(sink.astype(f32), qkv, qkv, qkv)


def _conv_kernel(xp_ref, xc_ref, xn_ref, w_ref, b_ref, xs_ref, bc_ref, *, nblk):
    j = pl.program_id(0) % nblk
    tm = xc_ref.shape[0]
    prev_ok = jnp.where(j >= 2, 1.0, 0.0)
    next_ok = jnp.where((j >= 1) & (j <= nblk - 2), 1.0, 0.0)
    ext = jnp.concatenate([xp_ref[...] * prev_ok, xc_ref[...], xn_ref[...] * next_ok], axis=0)
    n_ext = tm + 2 * CONV_HALO
    acc = jnp.zeros((tm, SSD_XBC_WIDTH), f32) + b_ref[...]
    for k in range(SSD_CONV):
        shifted = pltpu.roll(ext, (SSD_CONV // 2 - k) % n_ext, 0)[CONV_HALO:CONV_HALO + tm]
        acc = acc + shifted * w_ref[k:k + 1, :]
    y = _silu(acc)
    xs_ref[...] = y[:, 0:SSD_D_INNER]
    bc_ref[...] = y[:, SSD_D_INNER:].astype(bf16)


def _ssd_conv(xbc, conv_w, conv_b, t_per_batch):
    m = xbc.shape[0]
    tm = ROW_BLOCK
    nblk = t_per_batch // tm
    hb = tm // CONV_HALO
    last = m // CONV_HALO - 1
    return pl.pallas_call(
        functools.partial(_conv_kernel, nblk=nblk),
        grid=(m // tm,),
        in_specs=[
            pl.BlockSpec((CONV_HALO, SSD_XBC_WIDTH), lambda i: (jnp.maximum(i * hb - 1, 0), 0)),
            pl.BlockSpec((tm, SSD_XBC_WIDTH), lambda i: (i, 0)),
            pl.BlockSpec((CONV_HALO, SSD_XBC_WIDTH), lambda i: (jnp.minimum((i + 1) * hb, last), 0)),
            pl.BlockSpec((SSD_CONV, SSD_XBC_WIDTH), lambda i: (0, 0)),
            pl.BlockSpec((1, SSD_XBC_WIDTH), lambda i: (0, 0)),
        ],
        out_specs=[
            pl.BlockSpec((tm, SSD_D_INNER), lambda i: (i, 0)),
            pl.BlockSpec((tm, SSD_BC_WIDTH), lambda i: (i, 0)),
        ],
        out_shape=[
            jax.ShapeDtypeStruct((m, SSD_D_INNER), f32),
            jax.ShapeDtypeStruct((m, SSD_BC_WIDTH), bf16),
        ],
        compiler_params=_cparams(("arbitrary",)),
        name="ssd_conv",
    )(xbc, xbc, xbc, conv_w, conv_b.reshape(1, -1))


def _expand_heads(small, e_ref):
    hi = small.astype(bf16)
    lo = (small - hi.astype(f32)).astype(bf16)
    e = e_ref[...]
    return _dot(hi, e) + _dot(lo, e)


def _ssd_direction(fwd, xs_ref, bc_ref, dt_ref, dtT_ref, alog_ref, alogc_ref, dtb_ref, dtbc_ref, dsk_ref, e_ref,
                   y_ref, state_ref):
    di = 0 if fwd else 1
    lc = SSD_CHUNK
    nh = SSD_HEADS
    gw = SSD_GROUP_WIDTH
    hs = slice(di * nh, (di + 1) * nh)
    dt = _softplus(dt_ref[:, hs] + dtb_ref[:, hs])
    dtT = _softplus(dtT_ref[hs, :] + dtbc_ref[hs, :])
    a_row = -jnp.exp(alog_ref[di:di + 1, :])
    a_col = -jnp.exp(alogc_ref[di])
    d_row = dsk_ref[di:di + 1, :]
    da = dt * a_row
    daT = dtT * a_col
    ri = lax.broadcasted_iota(i32, (lc, lc), 0)
    ci = lax.broadcasted_iota(i32, (lc, lc), 1)
    tri = (ri >= ci) if fwd else (ri <= ci)
    trif = tri.astype(f32)
    acs = jnp.dot(trif, da, preferred_element_type=f32, precision=lax.Precision.HIGHEST)
    acsT = lax.dot_general(daT, trif, (((1,), (1,)), ((), ())), preferred_element_type=f32,
                           precision=lax.Precision.HIGHEST)
    tot = jnp.sum(da, axis=0, keepdims=True)
    small = jnp.concatenate([jnp.exp(acs), jnp.exp(tot - acs) * dt,
                             jnp.broadcast_to(jnp.exp(tot), (8, nh)), jnp.broadcast_to(d_row, (8, nh))], axis=0)
    big = _expand_heads(small, e_ref)
    eacs_x = big[0:lc]
    w_x = big[lc:2 * lc]
    dec_x = big[2 * lc:2 * lc + 1]
    dsk_x = big[2 * lc + 8:2 * lc + 9]

    x = xs_ref[...]
    xw = (x * w_x).astype(bf16)
    xb = x.astype(bf16)
    for g in range(SSD_GROUPS):
        lo = g * gw
        bg = bc_ref[:, g * SSD_D_STATE:(g + 1) * SSD_D_STATE]
        cg = bc_ref[:, (SSD_GROUPS + g) * SSD_D_STATE:(SSD_GROUPS + g + 1) * SSD_D_STATE]
        cb = _dot_nt(cg, bg)
        st = state_ref[di, g]
        y_inter = _dot(cg, st.astype(bf16))
        state_ref[di, g] = st * dec_x[:, lo:lo + gw] + _dot_tn(bg, xw[:, lo:lo + gw])
        ys = []
        for k in range(SSD_HEADS_PER_GROUP):
            h = g * SSD_HEADS_PER_GROUP + k
            seg = jnp.exp(jnp.where(tri, acs[:, h:h + 1] - acsT[h:h + 1, :], -jnp.inf))
            mat = (cb * seg * dtT[h:h + 1, :]).astype(bf16)
            ys.append(_dot(mat, xb[:, h * SSD_HEAD_DIM:(h + 1) * SSD_HEAD_DIM]))
        y_g = jnp.concatenate(ys, axis=-1) + y_inter * eacs_x[:, lo:lo + gw] + x[:, lo:lo + gw] * dsk_x[:, lo:lo + gw]
        y_ref[:, lo:lo + gw] = y_g.astype(y_ref.dtype)


def _ssd_kernel(xsf_ref, bcf_ref, dtf_ref, dtTf_ref, xsb_ref, bcb_ref, dtb_ref, dtTb_ref,
                alog_ref, alogc_ref, bias_ref, biasc_ref, dsk_ref, e_ref, yf_ref, yb_ref, state_ref):
    @pl.when(pl.program_id(1) == 0)
    def _():
        state_ref[...] = jnp.zeros_like(state_ref)

    params = (alog_ref, alogc_ref, bias_ref, biasc_ref, dsk_ref, e_ref)
    _ssd_direction(True, xsf_ref, bcf_ref, dtf_ref, dtTf_ref, *params, yf_ref, state_ref)
    _ssd_direction(False, xsb_ref, bcb_ref, dtb_ref, dtTb_ref, *params, yb_ref, state_ref)


def _ssd_scan(xs, bc, dt_raw, dtT_raw, a_log, dt_bias, d_skip, batch, t_per_batch):
    m = xs.shape[0]
    lc = SSD_CHUNK
    nch = t_per_batch // lc
    nctx = CTX_LEN // lc

    def fwd_chunk(b, j):
        return b * nch + j

    def bwd_chunk(b, j):
        return b * nch + jnp.where(j < nctx, nctx - 1 - j, nch - 1 + nctx - j)

    def streams(chunk):
        return [
            pl.BlockSpec((lc, SSD_D_INNER), lambda b, j: (chunk(b, j), 0)),
            pl.BlockSpec((lc, SSD_BC_WIDTH), lambda b, j: (chunk(b, j), 0)),
            pl.BlockSpec((lc, DT_WIDTH), lambda b, j: (chunk(b, j), 0)),
            pl.BlockSpec((DT_WIDTH, lc), lambda b, j: (0, chunk(b, j))),
        ]

    expand = (jnp.arange(SSD_D_INNER)[None, :] // SSD_HEAD_DIM == jnp.arange(SSD_HEADS)[:, None]).astype(bf16)
    const2 = lambda b, j: (0, 0)
    return pl.pallas_call(
        _ssd_kernel,
        grid=(batch, nch),
        in_specs=streams(fwd_chunk) + streams(bwd_chunk) + [
            pl.BlockSpec((2, SSD_HEADS), const2),
            pl.BlockSpec((2, SSD_HEADS, 1), lambda b, j: (0, 0, 0)),
            pl.BlockSpec((1, DT_WIDTH), const2),
            pl.BlockSpec((DT_WIDTH, 1), const2),
            pl.BlockSpec((2, SSD_HEADS), const2),
            pl.BlockSpec((SSD_HEADS, SSD_D_INNER), const2),
        ],
        out_specs=[
            pl.BlockSpec((lc, SSD_D_INNER), lambda b, j: (fwd_chunk(b, j), 0)),
            pl.BlockSpec((lc, SSD_D_INNER), lambda b, j: (bwd_chunk(b, j), 0)),
        ],
        out_shape=[jax.ShapeDtypeStruct((m, SSD_D_INNER), bf16), jax.ShapeDtypeStruct((m, SSD_D_INNER), bf16)],
        scratch_shapes=[pltpu.VMEM((2, SSD_GROUPS, SSD_D_STATE, SSD_GROUP_WIDTH), f32)],
        compiler_params=_cparams(("arbitrary", "arbitrary")),
        name="ssd_scan",
    )(xs, bc, dt_raw, dtT_raw, xs, bc, dt_raw, dtT_raw,
      a_log.astype(f32), a_log.astype(f32).reshape(2, SSD_HEADS, 1),
      dt_bias.astype(f32).reshape(1, DT_WIDTH), dt_bias.astype(f32).reshape(DT_WIDTH, 1), d_skip.astype(f32), expand)


def _outproj_kernel(oa_ref, ob_ref, yf_ref, yb_ref, z_ref, sgain_ref, w_ref, x_ref, mod_ref, ngain_ref, wrT_ref, rb_ref,
                    xmid_ref, hf_ref, e_ref, gate_ref, rank_ref, cnt_ref, carry_ref):
    d = D_MODEL
    u = (yf_ref[...].astype(f32) + yb_ref[...].astype(f32)) * _silu(z_ref[...].astype(f32))
    parts = []
    for g in range(SSD_GROUPS):
        ug = u[:, g * SSD_GROUP_WIDTH:(g + 1) * SSD_GROUP_WIDTH]
        parts.append(ug * lax.rsqrt(jnp.mean(ug * ug, axis=-1, keepdims=True) + EPS))
    gn = (jnp.concatenate(parts, axis=-1) * sgain_ref[...]).astype(bf16)
    acc = (_dot(oa_ref[...], w_ref[0:NA_WIDTH, :]) + _dot(ob_ref[...], w_ref[NA_WIDTH:NA_WIDTH + WA_WIDTH, :])
           + _dot(gn, w_ref[NA_WIDTH + WA_WIDTH:, :]))
    mod = mod_ref[0]
    xm = x_ref[...] + mod[:, 2 * d:3 * d] * acc
    xmid_ref[...] = xm
    hf = _rms_mod(xm, ngain_ref[...], mod[:, 3 * d:4 * d], mod[:, 4 * d:5 * d])
    hf_ref[...] = hf
    _route(hf, wrT_ref, rb_ref, e_ref, gate_ref, rank_ref, cnt_ref, carry_ref)


def _out_projection(o_a, o_b, y_f, y_b, z, ssd_norm, w_out, xa, mod4, ngain, w_router, router_bias, t_per_batch):
    m, d = xa.shape
    tm = ROW_BLOCK
    nblk = t_per_batch // tm
    row = lambda i: (i, 0)
    const = lambda i: (0, 0)
    mod_row = lambda i: (jnp.where(i % nblk == 0, 2, i // nblk), 0, 0)
    return pl.pallas_call(
        _outproj_kernel,
        grid=(m // tm,),
        in_specs=[
            pl.BlockSpec((tm, NA_WIDTH), row),
            pl.BlockSpec((tm, WA_WIDTH), row),
            pl.BlockSpec((tm, SSD_D_INNER), row),
            pl.BlockSpec((tm, SSD_D_INNER), row),
            pl.BlockSpec((tm, SSD_D_INNER), row),
            _resident((1, SSD_D_INNER), const),
            _resident((d, d), const),
            pl.BlockSpec((tm, d), row),
            pl.BlockSpec((1, 1, 6 * d), mod_row),
            _resident((1, d), const),
            _resident((N_EXPERTS, d), const),
            _resident((N_EXPERTS, 1), const),
        ],
        out_specs=[
            pl.BlockSpec((tm, d), row),
            pl.BlockSpec((tm, d), row),
            pl.BlockSpec((2, tm), lambda i: (0, i)),
            pl.BlockSpec((2, tm), lambda i: (0, i)),
            pl.BlockSpec((2, tm), lambda i: (0, i)),
            pl.BlockSpec((N_EXPERTS, 128), const),
        ],
        out_shape=[
            jax.ShapeDtypeStruct((m, d), f32),
            jax.ShapeDtypeStruct((m, d), f32),
            jax.ShapeDtypeStruct((2, m), i32),
            jax.ShapeDtypeStruct((2, m), f32),
            jax.ShapeDtypeStruct((2, m), i32),
            jax.ShapeDtypeStruct((N_EXPERTS, 128), i32),
        ],
        scratch_shapes=[pltpu.VMEM((N_EXPERTS, 1), f32)],
        compiler_params=_cparams(("arbitrary",)),
        name="out_projection",
    )(o_a, o_b, y_f, y_b, z, ssd_norm.reshape(1, -1), w_out.astype(bf16), xa, mod4, ngain.reshape(1, d),
      w_router.T.astype(bf16), router_bias.astype(f32).reshape(N_EXPERTS, 1))


def _route(hf, wrT_ref, rb_ref, e_ref, gate_ref, rank_ref, cnt_ref, carry_ref):
    i = pl.program_id(0)
    tm = hf.shape[0]
    ng, ge = N_EXPERT_GROUPS, EXPERTS_PER_GROUP

    @pl.when(i == 0)
    def _():
        carry_ref[...] = jnp.zeros_like(carry_ref)

    aff = _sigmoid(_dot_nt(wrT_ref[...], hf.astype(bf16)))
    sel3 = (aff + rb_ref[...]).reshape(ng, ge, tm)
    aff3 = aff.reshape(ng, ge, tm)
    io = lax.broadcasted_iota(i32, (ng, ge, tm), 1)
    m1 = jnp.max(sel3, axis=1, keepdims=True)
    i1 = jnp.min(jnp.where(sel3 == m1, io, ge), axis=1, keepdims=True)
    rest = jnp.where(io == i1, -jnp.inf, sel3)
    m2 = jnp.max(rest, axis=1, keepdims=True)
    i2 = jnp.min(jnp.where(rest == m2, io, ge), axis=1, keepdims=True)
    a1 = jnp.sum(jnp.where(io == i1, aff3, 0.0), axis=1)
    a2 = jnp.sum(jnp.where(io == i2, aff3, 0.0), axis=1)
    score = (m1 + m2)[:, 0, :]
    gi = lax.broadcasted_iota(i32, (ng, tm), 0)
    best = jnp.max(score, axis=0, keepdims=True)
    gb = jnp.min(jnp.where(score == best, gi, ng), axis=0, keepdims=True)
    picked = gi == gb
    pick_i = lambda a: jnp.sum(jnp.where(picked, a, 0), axis=0, keepdims=True)
    pick_f = lambda a: jnp.sum(jnp.where(picked, a, 0.0), axis=0, keepdims=True)
    e1 = gb * ge + pick_i(i1[:, 0, :])
    e2 = gb * ge + pick_i(i2[:, 0, :])
    g1 = pick_f(a1)
    g2 = pick_f(a2)
    den = g1 + g2
    e_ref[...] = jnp.concatenate([e1, e2], axis=0)
    gate_ref[...] = jnp.concatenate([g1 / den, g2 / den], axis=0)

    ei = lax.broadcasted_iota(i32, (N_EXPERTS, tm), 0)
    o1 = jnp.where(ei == e1, 1.0, 0.0)
    o2 = jnp.where(ei == e2, 1.0, 0.0)
    both = o1 + o2
    upper = jnp.where(lax.broadcasted_iota(i32, (tm, tm), 0) <= lax.broadcasted_iota(i32, (tm, tm), 1), 1.0, 0.0)
    incl = _dot(both.astype(bf16), upper.astype(bf16))
    before = incl - both + carry_ref[...]
    r1 = jnp.sum(o1 * before, axis=0, keepdims=True)
    r2 = jnp.sum(o2 * before, axis=0, keepdims=True)
    rank_ref[...] = jnp.concatenate([r1, r2], axis=0).astype(i32)
    total = carry_ref[...] + jnp.sum(both, axis=1, keepdims=True)
    carry_ref[...] = total
    cnt_ref[...] = jnp.broadcast_to(total, cnt_ref.shape).astype(i32)


def _dispatch_kernel(last_block_ref, tail_ref, slot_ref, hf_ref, xbuf_ref, zero_ref, sem, zero_sem):
    tm = slot_ref.shape[1]

    @pl.when(pl.program_id(0) == 0)
    def _():
        zero_ref[...] = jnp.zeros_like(zero_ref)

        def block_copy(b):
            r0 = pl.multiple_of(b * MOE_BLOCK, MOE_BLOCK)
            return pltpu.make_async_copy(zero_ref, xbuf_ref.at[pl.ds(r0, MOE_BLOCK)], zero_sem)

        def per_expert(fn):
            def body(e, carry):
                @pl.when(last_block_ref[e] >= 0)
                def _():
                    fn(last_block_ref[e])
                return carry
            lax.fori_loop(0, N_EXPERTS, body, 0)

        def per_tail(fn):
            def body(b, carry):
                fn(b)
                return carry
            lax.fori_loop(tail_ref[0], tail_ref[1], body, 0)

        per_expert(lambda b: block_copy(b).start())
        per_tail(lambda b: block_copy(b).start())
        per_expert(lambda b: block_copy(0).wait())
        per_tail(lambda b: block_copy(0).wait())

    def row_copy(r, s):
        return pltpu.make_async_copy(hf_ref.at[pl.ds(r, 1)], xbuf_ref.at[pl.ds(s, 1)], sem)

    def issue(q, carry):
        for u in range(ROW_UNROLL):
            r = q * ROW_UNROLL + u
            for k in range(2):
                row_copy(r, slot_ref[k, r]).start()
        return carry

    lax.fori_loop(0, tm // ROW_UNROLL, issue, 0)

    def drain(q, carry):
        for _ in range(2 * ROW_UNROLL):
            row_copy(0, 0).wait()
        return carry

    lax.fori_loop(0, tm // ROW_UNROLL, drain, 0)


def _dispatch(hf, slot, last_block, tail, n_pad):
    m, d = hf.shape
    tm = ROW_BLOCK
    return pl.pallas_call(
        _dispatch_kernel,
        grid_spec=pltpu.PrefetchScalarGridSpec(
            num_scalar_prefetch=2,
            grid=(m // tm,),
            in_specs=[
                pl.BlockSpec((2, tm), lambda i, lb, tl: (0, i), memory_space=pltpu.SMEM),
                pl.BlockSpec((tm, d), lambda i, lb, tl: (i, 0)),
            ],
            out_specs=pl.BlockSpec(memory_space=pl.ANY),
            scratch_shapes=[pltpu.VMEM((MOE_BLOCK, d), f32), pltpu.SemaphoreType.DMA, pltpu.SemaphoreType.DMA],
        ),
        out_shape=jax.ShapeDtypeStruct((n_pad, d), f32),
        compiler_params=_cparams(("arbitrary",)),
        name="moe_dispatch",
    )(last_block, tail, slot, hf)


def _block_state(be_ref, nb_ref):
    i = pl.program_id(0)
    used = i < nb_ref[0]
    fresh = jnp.logical_or(i == 0, be_ref[i] != be_ref[jnp.maximum(i - 1, 0)])
    return used, jnp.logical_and(used, fresh)


def _cast_and_dot(x, w_ref, wb_ref):
    k = wb_ref.shape[0]
    ck = MOE_CAST_ROWS
    acc = None
    for r0 in range(0, k, ck):
        wc = w_ref[0, r0:r0 + ck, :].astype(bf16)
        wb_ref[r0:r0 + ck, :] = wc
        part = _dot(x[:, r0:r0 + ck], wc)
        acc = part if acc is None else acc + part
    return acc


def _moe_up_kernel(be_ref, nb_ref, x_ref, wg_ref, wu_ref, h_ref, wgb_ref, wub_ref):
    used, fresh = _block_state(be_ref, nb_ref)

    def finish(g, u):
        h_ref[...] = (_silu(g) * u).astype(bf16)

    @pl.when(fresh)
    def _():
        x = x_ref[...].astype(bf16)
        finish(_cast_and_dot(x, wg_ref, wgb_ref), _cast_and_dot(x, wu_ref, wub_ref))

    @pl.when(jnp.logical_and(used, jnp.logical_not(fresh)))
    def _():
        x = x_ref[...].astype(bf16)
        finish(_dot(x, wgb_ref[...]), _dot(x, wub_ref[...]))

    @pl.when(jnp.logical_not(used))
    def _():
        h_ref[...] = jnp.zeros_like(h_ref)


def _moe_down_kernel(be_ref, nb_ref, h_ref, wd_ref, y_ref, wdb_ref):
    used, fresh = _block_state(be_ref, nb_ref)

    @pl.when(fresh)
    def _():
        y_ref[...] = _cast_and_dot(h_ref[...], wd_ref, wdb_ref)

    @pl.when(jnp.logical_and(used, jnp.logical_not(fresh)))
    def _():
        y_ref[...] = _dot(h_ref[...], wdb_ref[...])

    @pl.when(jnp.logical_not(used))
    def _():
        y_ref[...] = jnp.zeros_like(y_ref)


def _moe_experts(xbuf, block_expert, n_blocks_used, wg, wu, wd, layer):
    n_pad, d = xbuf.shape
    tme = MOE_BLOCK
    f = wg.shape[-1]
    blk = lambda i, be, nb: (jnp.minimum(i, nb[0] - 1), 0)
    own = lambda i, be, nb: (i, 0)
    wsel = lambda i, be, nb: (layer, be[i], 0, 0)
    h = pl.pallas_call(
        _moe_up_kernel,
        grid_spec=pltpu.PrefetchScalarGridSpec(
            num_scalar_prefetch=2,
            grid=(n_pad // tme,),
            in_specs=[
                pl.BlockSpec((tme, d), blk),
                pl.BlockSpec((None, 1, d, f), wsel),
                pl.BlockSpec((None, 1, d, f), wsel),
            ],
            out_specs=pl.BlockSpec((tme, f), own),
            scratch_shapes=[pltpu.VMEM((d, f), bf16), pltpu.VMEM((d, f), bf16)],
        ),
        out_shape=jax.ShapeDtypeStruct((n_pad, f), bf16),
        compiler_params=_cparams(("arbitrary",)),
        name="moe_up",
    )(block_expert, n_blocks_used, xbuf, wg, wu)
    return pl.pallas_call(
        _moe_down_kernel,
        grid_spec=pltpu.PrefetchScalarGridSpec(
            num_scalar_prefetch=2,
            grid=(n_pad // tme,),
            in_specs=[
                pl.BlockSpec((tme, f), blk),
                pl.BlockSpec((None, 1, f, d), wsel),
            ],
            out_specs=pl.BlockSpec((tme, d), own),
            scratch_shapes=[pltpu.VMEM((f, d), bf16)],
        ),
        out_shape=jax.ShapeDtypeStruct((n_pad, d), f32),
        compiler_params=_cparams(("arbitrary",)),
        name="moe_down",
    )(block_expert, n_blocks_used, h, wd)


def _combine_kernel(slot_ref, slot_next_ref, ybuf_ref, gate_ref, x_ref, mod_ref, fgain_ref, o_ref, buf_ref, sems, *,
                    final_norm, n_steps):
    s = pl.program_id(0)
    tm = x_ref.shape[0]
    d = D_MODEL
    cur = s % 2

    def row_copy(b, k, r, src):
        return pltpu.make_async_copy(ybuf_ref.at[pl.ds(src, 1)], buf_ref.at[b, k, pl.ds(r, 1)], sems.at[b])

    def gather(rows_ref, b):
        def issue(q, carry):
            for u in range(ROW_UNROLL):
                r = q * ROW_UNROLL + u
                for k in range(2):
                    row_copy(b, k, r, rows_ref[k, r]).start()
            return carry

        lax.fori_loop(0, tm // ROW_UNROLL, issue, 0)

    @pl.when(s == 0)
    def _():
        gather(slot_ref, 0)

    @pl.when(s + 1 < n_steps)
    def _():
        gather(slot_next_ref, 1 - cur)

    def drain(q, carry):
        for _ in range(ROW_UNROLL):
            for k in range(2):
                row_copy(cur, k, 0, 0).wait()
        return carry

    lax.fori_loop(0, tm // ROW_UNROLL, drain, 0)
    gate = gate_ref[...]
    y = buf_ref[cur, 0] * gate[:, 0:1] + buf_ref[cur, 1] * gate[:, 1:2]
    xn = x_ref[...] + mod_ref[0][:, 5 * d:6 * d] * y
    if final_norm:
        xn = (xn * lax.rsqrt(jnp.mean(xn * xn, axis=-1, keepdims=True) + EPS)) * fgain_ref[...]
    o_ref[...] = xn


def _combine(ybuf, slot, gate_t, xmid, mod4, fgain, batch, t_per_batch, latent_only):
    m, d = xmid.shape
    tm = ROW_BLOCK
    nblk = t_per_batch // tm
    if latent_only:
        per_batch = nblk - 1
        src = lambda s: (s // per_batch) * nblk + 1 + s % per_batch
        modr = lambda s: s // per_batch
    else:
        per_batch = nblk
        src = lambda s: s
        modr = lambda s: jnp.where(s % nblk == 0, 2, s // nblk)
    n_steps = batch * per_batch
    nxt = lambda s: src(jnp.minimum(s + 1, n_steps - 1))
    return pl.pallas_call(
        functools.partial(_combine_kernel, final_norm=latent_only, n_steps=n_steps),
        grid=(n_steps,),
        in_specs=[
            pl.BlockSpec((2, tm), lambda s: (0, src(s)), memory_space=pltpu.SMEM),
            pl.BlockSpec((2, tm), lambda s: (0, nxt(s)), memory_space=pltpu.SMEM),
            pl.BlockSpec(memory_space=pl.ANY),
            pl.BlockSpec((tm, 2), lambda s: (src(s), 0)),
            pl.BlockSpec((tm, d), lambda s: (src(s), 0)),
            pl.BlockSpec((1, 1, 6 * d), lambda s: (modr(s), 0, 0)),
            pl.BlockSpec((1, d), lambda s: (0, 0)),
        ],
        out_specs=pl.BlockSpec((tm, d), lambda s: (s, 0)),
        out_shape=jax.ShapeDtypeStruct((n_steps * tm, d), f32),
        scratch_shapes=[pltpu.VMEM((2, 2, tm, d), f32), pltpu.SemaphoreType.DMA((2,))],
        compiler_params=_cparams(("arbitrary",)),
        name="moe_combine",
    )(slot, slot, ybuf, gate_t, xmid, mod4, fgain.reshape(1, d))


def _moe_plan(expert, rank, counts, n_assign):
    tme = MOE_BLOCK
    n_blocks = -(-n_assign // tme) + N_EXPERTS
    blocks_per_expert = (counts + tme - 1) // tme
    block_end = jnp.cumsum(blocks_per_expert)
    row_start = (block_end - blocks_per_expert) * tme
    chosen = expert[..., None] == jnp.arange(N_EXPERTS)
    slot = rank + jnp.sum(jnp.where(chosen, row_start, 0), axis=-1)
    block_expert = jnp.minimum(jnp.sum(block_end[None, :] <= jnp.arange(n_blocks)[:, None], axis=1), N_EXPERTS - 1)
    last_block = jnp.where(blocks_per_expert > 0, block_end - 1, -1)
    tail = jnp.stack([block_end[-1], jnp.asarray(n_blocks, block_end.dtype)])
    return (slot.astype(i32), block_expert.astype(i32), block_end[-1:].astype(i32), last_block.astype(i32),
            tail.astype(i32), n_blocks * tme)


def kernel(x, c, ctx, c_ctx, w_ada, b_ada, norm_mix, norm_ffn, norm_final, w_in, w_out, na_rpb, wa_sink,
           ssd_conv_w, ssd_conv_b, ssd_a_log, ssd_dt_bias, ssd_d, ssd_norm, w_router, router_bias,
           w_gate, w_up, w_down):
    batch, seq, d = x.shape
    depth = w_ada.shape[0]
    assert d == D_MODEL and ctx.shape[1] == CTX_LEN and seq % ROW_BLOCK == 0 and batch <= 2
    t_per_batch = CTX_LEN + seq
    m = batch * t_per_batch
    rows = seq // GRID_W

    xa = jnp.concatenate([ctx, x], axis=1).reshape(m, d)
    cc = jnp.zeros((8, d), f32).at[0:batch].set(c).at[2].set(c_ctx)
    mod_all = _ada_modulation(cc, w_ada, b_ada).reshape(depth, 8, 1, 6 * d)
    rope_tabs = _rope_tables(seq)

    out = None
    for l in range(depth):
        mod4 = mod_all[l]
        qkv, z, xbc, dt_raw, dtT_raw = _in_projection(xa, mod4, norm_mix[l], w_in[l], rope_tabs, t_per_batch)
        o_a = _neighbourhood_attention(qkv, _na_bias_tables(na_rpb[l], rows), batch, t_per_batch)
        o_b = _window_attention(qkv, wa_sink[l], batch, t_per_batch)
        xs, bc = _ssd_conv(xbc, ssd_conv_w[l], ssd_conv_b[l], t_per_batch)
        y_f, y_b = _ssd_scan(xs, bc, dt_raw, dtT_raw, ssd_a_log[l], ssd_dt_bias[l], ssd_d[l], batch, t_per_batch)
        xmid, hf, expert, gate, rank, cnt = _out_projection(o_a, o_b, y_f, y_b, z, ssd_norm[l], w_out[l], xa, mod4,
                                                            norm_ffn[l], w_router, router_bias, t_per_batch)
        slot, block_expert, n_used, last_block, tail, n_pad = _moe_plan(expert, rank, cnt[:, 0], 2 * m)
        xbuf = _dispatch(hf, slot, last_block, tail, n_pad)
        ybuf = _moe_experts(xbuf, block_expert, n_used, w_gate, w_up, w_down, l)
        last = l == depth - 1
        res = _combine(ybuf, slot, gate.T, xmid, mod4, norm_final, batch, t_per_batch, latent_only=last)
        if last:
            out = res.reshape(batch, seq, d)
        else:
            xa = res
    return out
```

```python
import functools

import jax
import jax.numpy as jnp
import numpy as np
from jax import lax
from jax.experimental import pallas as pl
from jax.experimental.pallas import tpu as pltpu

f32 = jnp.float32
bf16 = jnp.bfloat16
i32 = jnp.int32

D_MODEL = 2048
GRID_W = 64
CTX_LEN = 256
EPS = 1e-6
MASK_VALUE = -1e30
HEAD_DIM = 128
NA_WIDTH = D_MODEL // 4
NA_HEADS = NA_WIDTH // HEAD_DIM
NA_WIN_R = 8
NA_WIN_C = 16
WA_WIDTH = D_MODEL // 4
WA_HEADS = WA_WIDTH // HEAD_DIM
WA_KV_HEADS = 2
WA_KV_WIDTH = WA_KV_HEADS * HEAD_DIM
WA_WINDOW = 128
WA_BLOCK = 128
ROPE_BASE = 10000.0
SSD_D_INNER = D_MODEL // 2
SSD_HEAD_DIM = 64
SSD_HEADS = SSD_D_INNER // SSD_HEAD_DIM
SSD_GROUPS = 2
SSD_D_STATE = 128
SSD_CONV = 5
SSD_BC_WIDTH = 2 * SSD_GROUPS * SSD_D_STATE
SSD_XBC_WIDTH = SSD_D_INNER + SSD_BC_WIDTH
SSD_GROUP_WIDTH = SSD_D_INNER // SSD_GROUPS
SSD_HEADS_PER_GROUP = SSD_HEADS // SSD_GROUPS
N_EXPERTS = 32
N_EXPERT_GROUPS = 4
EXPERTS_PER_GROUP = N_EXPERTS // N_EXPERT_GROUPS
D_FF_EXPERT = D_MODEL // 2

QKV_WIDTH = 3 * NA_WIDTH + WA_WIDTH + 2 * WA_KV_WIDTH
ROPE_LO = 3 * NA_WIDTH
ROPE_HI = ROPE_LO + WA_WIDTH + WA_KV_WIDTH
Z_LO = QKV_WIDTH
XBC_LO = Z_LO + SSD_D_INNER
DT_LO = XBC_LO + SSD_XBC_WIDTH
DT_WIDTH = 2 * SSD_HEADS

ROW_BLOCK = 256
SSD_CHUNK = 128
NA_GROUP_ROWS = 4
WA_STEP_BLOCKS = 2
NA_KEY_ROWS = NA_GROUP_ROWS + NA_WIN_R
MOE_BLOCK = 256
MOE_CAST_ROWS = 256
ROW_UNROLL = 8
CONV_HALO = 8
VMEM_LIMIT = 56 * 1024 * 1024


def _cparams(sem):
    return pltpu.CompilerParams(dimension_semantics=sem, vmem_limit_bytes=VMEM_LIMIT)


def _sigmoid(x):
    return 1.0 / (1.0 + jnp.exp(-x))


def _silu(x):
    return x * _sigmoid(x)


def _softplus(x):
    return jnp.maximum(x, 0.0) + jnp.log1p(jnp.exp(-jnp.abs(x)))


def _dot(a, b):
    return jnp.dot(a, b, preferred_element_type=f32)


def _dot_nt(a, b):
    return lax.dot_general(a, b, (((1,), (1,)), ((), ())), preferred_element_type=f32)


def _dot_tn(a, b):
    return lax.dot_general(a, b, (((0,), (0,)), ((), ())), preferred_element_type=f32)


def _resident(shape, index_map):
    return pl.BlockSpec(shape, index_map, pipeline_mode=pl.Buffered(1))


def _ada_kernel(c_ref, w_ref, b_ref, o_ref):
    s = _silu(c_ref[...]).astype(bf16)
    o_ref[0] = _dot(s, w_ref[0].astype(bf16)) + b_ref[0]


def _ada_modulation(cc, w_ada, b_ada):
    depth, d, n = w_ada.shape
    tn = 1024
    return pl.pallas_call(
        _ada_kernel,
        grid=(depth, n // tn),
        in_specs=[
            pl.BlockSpec((8, d), lambda l, j: (0, 0)),
            pl.BlockSpec((1, d, tn), lambda l, j: (l, 0, j)),
            pl.BlockSpec((1, 1, tn), lambda l, j: (l, 0, j)),
        ],
        out_specs=pl.BlockSpec((1, 8, tn), lambda l, j: (l, 0, j)),
        out_shape=jax.ShapeDtypeStruct((depth, 8, n), f32),
        compiler_params=_cparams(("arbitrary", "arbitrary")),
        name="ada_modulation",
    )(cc, w_ada, b_ada.reshape(depth, 1, n))


def _rms_mod(x, gain, shift, scale):
    y = x * lax.rsqrt(jnp.mean(x * x, axis=-1, keepdims=True) + EPS)
    return (y * gain) * (1.0 + scale) + shift


def _inproj_kernel(x_ref, mod_ref, gain_ref, wqkv_ref, wz_ref, wxbc_ref, wdt_ref, wdtT_ref,
                   cos_ref, sina_ref, sinb_ref, qkv_ref, z_ref, xbc_ref, dt_ref, dtT_ref):
    d = D_MODEL
    mod = mod_ref[0]
    h = _rms_mod(x_ref[...], gain_ref[...], mod[:, 0:d], mod[:, d:2 * d]).astype(bf16)
    acc = _dot(h, wqkv_ref[...])
    qkv_ref[:, 0:ROPE_LO] = acc[:, 0:ROPE_LO].astype(bf16)
    cos, sina, sinb = cos_ref[...], sina_ref[...], sinb_ref[...]
    quarter = HEAD_DIM // 4
    for c0 in range(ROPE_LO, ROPE_HI, HEAD_DIM):
        xh = acc[:, c0:c0 + HEAD_DIM]
        rot = (xh * cos + pltpu.roll(xh, HEAD_DIM - quarter, 1) * sina + pltpu.roll(xh, quarter, 1) * sinb)
        qkv_ref[:, c0:c0 + HEAD_DIM] = rot.astype(bf16)
    qkv_ref[:, ROPE_HI:QKV_WIDTH] = acc[:, ROPE_HI:QKV_WIDTH].astype(bf16)
    z_ref[...] = _dot(h, wz_ref[...]).astype(z_ref.dtype)
    xbc_ref[...] = _dot(h, wxbc_ref[...])
    dt_ref[...] = _dot(h, wdt_ref[...])
    dtT_ref[0] = _dot_nt(wdtT_ref[...], h)


def _in_projection(xa, mod4, gain, w_in, rope_tabs, t_per_batch):
    m, d = xa.shape
    tm = ROW_BLOCK
    nblk = t_per_batch // tm
    wb = w_in.astype(bf16)
    wqkv = wb[:, 0:QKV_WIDTH]
    wz = wb[:, Z_LO:XBC_LO]
    wxbc = wb[:, XBC_LO:DT_LO]
    wdt = wb[:, DT_LO:DT_LO + DT_WIDTH]
    wdtT = wdt.T
    cos, sina, sinb = rope_tabs
    row = lambda i: (i, 0)
    const = lambda i: (0, 0)
    mod_row = lambda i: (jnp.where(i % nblk == 0, 2, i // nblk), 0, 0)
    tab_row = lambda i: (i % nblk, 0)
    return pl.pallas_call(
        _inproj_kernel,
        grid=(m // tm,),
        in_specs=[
            pl.BlockSpec((tm, d), row),
            pl.BlockSpec((1, 1, 6 * d), mod_row),
            _resident((1, d), const),
            _resident((d, QKV_WIDTH), const),
            _resident((d, SSD_D_INNER), const),
            _resident((d, SSD_XBC_WIDTH), const),
            _resident((d, DT_WIDTH), const),
            _resident((DT_WIDTH, d), const),
            pl.BlockSpec((tm, HEAD_DIM), tab_row),
            pl.BlockSpec((tm, HEAD_DIM), tab_row),
            pl.BlockSpec((tm, HEAD_DIM), tab_row),
        ],
        out_specs=[
            pl.BlockSpec((tm, QKV_WIDTH), row),
            pl.BlockSpec((tm, SSD_D_INNER), row),
            pl.BlockSpec((tm, SSD_XBC_WIDTH), row),
            pl.BlockSpec((tm, DT_WIDTH), row),
            pl.BlockSpec((1, DT_WIDTH, tm), lambda i: (i // nblk, 0, i % nblk)),
        ],
        out_shape=[
            jax.ShapeDtypeStruct((m, QKV_WIDTH), bf16),
            jax.ShapeDtypeStruct((m, SSD_D_INNER), bf16),
            jax.ShapeDtypeStruct((m, SSD_XBC_WIDTH), f32),
            jax.ShapeDtypeStruct((m, DT_WIDTH), f32),
            jax.ShapeDtypeStruct((m // t_per_batch, DT_WIDTH, t_per_batch), f32),
        ],
        compiler_params=_cparams(("arbitrary",)),
        name="in_projection",
    )(xa, mod4, gain.reshape(1, d), wqkv, wz, wxbc, wdt, wdtT, cos, sina, sinb)


def _rope_tables(seq):
    t = jnp.arange(seq, dtype=i32)
    q = HEAD_DIM // 4
    inv = jnp.power(ROPE_BASE, -jnp.arange(q, dtype=f32) / q)
    ang_r = (t // GRID_W).astype(f32)[:, None] * inv
    ang_c = (t % GRID_W).astype(f32)[:, None] * inv
    zero = jnp.zeros_like(ang_r)
    cos = jnp.concatenate([jnp.cos(ang_r), jnp.cos(ang_r), jnp.cos(ang_c), jnp.cos(ang_c)], axis=-1)
    sina = jnp.concatenate([-jnp.sin(ang_r), zero, -jnp.sin(ang_c), zero], axis=-1)
    sinb = jnp.concatenate([zero, jnp.sin(ang_r), zero, jnp.sin(ang_c)], axis=-1)
    pad = lambda a, v: jnp.concatenate([jnp.full((CTX_LEN, HEAD_DIM), v, f32), a], axis=0)
    return pad(cos, 1.0), pad(sina, 0.0), pad(sinb, 0.0)


def _na_bias_tables(rpb, rows):
    col = np.arange(GRID_W)
    c0 = np.clip(col - NA_WIN_C // 2, 0, GRID_W - NA_WIN_C)
    col_ok = (col[None, :] >= c0[:, None]) & (col[None, :] < c0[:, None] + NA_WIN_C)
    dc = np.clip(col[None, :] - col[:, None] + NA_WIN_C - 1, 0, 2 * NA_WIN_C - 2)
    dc_hot = (dc[:, :, None] == np.arange(2 * NA_WIN_C - 1)).astype(np.float32)
    exact = lax.Precision.HIGHEST
    by_col = jnp.einsum("hde,qke->hdqk", rpb.astype(f32), dc_hot, precision=exact)
    tabs = []
    for r_first in (NA_GROUP_ROWS, 0, rows - NA_GROUP_ROWS):
        start = min(max(r_first - NA_WIN_R // 2, 0), rows - NA_KEY_ROWS)
        r = r_first + np.arange(NA_GROUP_ROWS)
        kr = start + np.arange(NA_KEY_ROWS)
        r0 = np.clip(r - NA_WIN_R // 2, 0, rows - NA_WIN_R)
        row_ok = (kr[None, :] >= r0[:, None]) & (kr[None, :] < r0[:, None] + NA_WIN_R)
        dr = np.clip(kr[None, :] - r[:, None] + NA_WIN_R - 1, 0, 2 * NA_WIN_R - 2)
        dr_hot = (dr[:, :, None] == np.arange(2 * NA_WIN_R - 1)).astype(np.float32)
        b = jnp.einsum("gid,hdqk->hgqik", dr_hot, by_col, precision=exact)
        ok = row_ok[:, None, :, None] & col_ok[None, :, None, :]
        b = jnp.where(ok[None], b, MASK_VALUE)
        tabs.append(b.reshape(rpb.shape[0], NA_GROUP_ROWS * GRID_W, NA_KEY_ROWS * GRID_W))
    return jnp.stack(tabs, axis=1)


def _na_kernel(q_ref, k_ref, v_ref, bias_ref, o_ref, *, rows):
    g = pl.program_id(1)
    scale = HEAD_DIM ** -0.5
    nk = NA_KEY_ROWS * GRID_W

    def head_cols(h):
        return slice(h * HEAD_DIM, (h + 1) * HEAD_DIM)

    @pl.when(g == 0)
    def _():
        for h in range(NA_HEADS):
            hc = head_cols(h)
            s_ctx = _dot_nt(q_ref[:, hc], k_ref[0:CTX_LEN, hc]) * scale
            m = jnp.max(s_ctx, axis=-1, keepdims=True)
            p = jnp.exp(s_ctx - m)
            l = jnp.sum(p, axis=-1, keepdims=True)
            o_ref[:, hc] = (_dot(p.astype(bf16), v_ref[0:CTX_LEN, hc]) / l).astype(bf16)

    @pl.when(g > 0)
    def _():
        r_first = (g - 1) * NA_GROUP_ROWS
        start = jnp.clip(r_first - NA_WIN_R // 2, 0, rows - NA_KEY_ROWS)
        off = pl.multiple_of(CTX_LEN + start * GRID_W, GRID_W)
        for h in range(NA_HEADS):
            hc = head_cols(h)
            q = q_ref[:, hc]
            vc = v_ref[0:CTX_LEN, hc]
            s_ctx = _dot_nt(q, k_ref[0:CTX_LEN, hc]) * scale
            s_loc = _dot_nt(q, k_ref[pl.ds(off, nk), hc]) * scale + bias_ref[h, 0]
            m = jnp.maximum(jnp.max(s_loc, axis=-1, keepdims=True), jnp.max(s_ctx, axis=-1, keepdims=True))
            p_loc = jnp.exp(s_loc - m)
            p_ctx = jnp.exp(s_ctx - m)
            l = jnp.sum(p_loc, axis=-1, keepdims=True) + jnp.sum(p_ctx, axis=-1, keepdims=True)
            o = _dot(p_loc.astype(bf16), v_ref[pl.ds(off, nk), hc]) + _dot(p_ctx.astype(bf16), vc)
            o_ref[:, hc] = (o / l).astype(bf16)


def _neighbourhood_attention(qkv, bias_tabs, batch, t_per_batch):
    m = qkv.shape[0]
    rows = (t_per_batch - CTX_LEN) // GRID_W
    tq = NA_GROUP_ROWS * GRID_W
    assert tq == CTX_LEN and rows % NA_GROUP_ROWS == 0 and rows >= NA_KEY_ROWS + NA_GROUP_ROWS
    ng = rows // NA_GROUP_ROWS
    nblk = t_per_batch // tq
    bias_idx = lambda b, g: (0, jnp.where(g == 1, 1, jnp.where(g == ng, 2, 0)), 0, 0)
    kv_spec = lambda col: pl.BlockSpec((t_per_batch, NA_WIDTH), lambda b, g: (b, col), pipeline_mode=pl.Buffered(1))
    return pl.pallas_call(
        functools.partial(_na_kernel, rows=rows),
        grid=(batch, ng + 1),
        in_specs=[
            pl.BlockSpec((tq, NA_WIDTH), lambda b, g: (b * nblk + g, 0)),
            kv_spec(1),
            kv_spec(2),
            pl.BlockSpec((NA_HEADS, 1, tq, NA_KEY_ROWS * GRID_W), bias_idx),
        ],
        out_specs=pl.BlockSpec((tq, NA_WIDTH), lambda b, g: (b * nblk + g, 0)),
        out_shape=jax.ShapeDtypeStruct((m, NA_WIDTH), bf16),
        compiler_params=_cparams(("arbitrary", "arbitrary")),
        name="neighbourhood_attention",
    )(qkv, qkv, qkv, bias_tabs)


def _wa_kernel(sink_ref, q_ref, k_ref, v_ref, o_ref, *, nb):
    j = pl.program_id(1)
    scale = HEAD_DIM ** -0.5
    tb = WA_BLOCK
    g = WA_HEADS // WA_KV_HEADS
    nk = tb + 2 * WA_WINDOW
    ctx_steps = CTX_LEN // (tb * WA_STEP_BLOCKS)
    rowi = lax.broadcasted_iota(i32, (g * tb, 1), 0)

    def stacked_q(sub, kh):
        rows = slice(sub * tb, (sub + 1) * tb)
        return jnp.concatenate([q_ref[rows, (kh * g + i) * HEAD_DIM:(kh * g + i + 1) * HEAD_DIM] for i in range(g)], axis=0)

    def sink_col(kh):
        sink = jnp.full((g * tb, 1), sink_ref[kh * g], f32)
        for i in range(1, g):
            sink = jnp.where(rowi >= i * tb, sink_ref[kh * g + i], sink)
        return sink

    def finish(sub, kh, o):
        for i in range(g):
            o_ref[sub * tb:(sub + 1) * tb, (kh * g + i) * HEAD_DIM:(kh * g + i + 1) * HEAD_DIM] = (
                o[i * tb:(i + 1) * tb].astype(bf16))

    @pl.when(j < ctx_steps)
    def _():
        for sub in range(WA_STEP_BLOCKS):
            for kh in range(WA_KV_HEADS):
                kcols = slice(kh * HEAD_DIM, (kh + 1) * HEAD_DIM)
                sink = sink_col(kh)
                s_ctx = _dot_nt(stacked_q(sub, kh), k_ref[0:CTX_LEN, kcols]) * scale
                m = jnp.maximum(jnp.max(s_ctx, axis=-1, keepdims=True), sink)
                p = jnp.exp(s_ctx - m)
                l = jnp.sum(p, axis=-1, keepdims=True) + jnp.exp(sink - m)
                finish(sub, kh, _dot(p.astype(bf16), v_ref[0:CTX_LEN, kcols]) / l)

    @pl.when(j >= ctx_steps)
    def _():
        for sub in range(WA_STEP_BLOCKS):
            n = (j - ctx_steps) * WA_STEP_BLOCKS + sub
            st = jnp.clip(n * tb - WA_WINDOW, 0, nb * tb - nk)
            off = pl.multiple_of(CTX_LEN + st, tb)
            qpos = n * tb + lax.broadcasted_iota(i32, (g * tb, nk), 0) % tb
            kpos = st + lax.broadcasted_iota(i32, (g * tb, nk), 1)
            ok = jnp.abs(qpos - kpos) <= WA_WINDOW
            for kh in range(WA_KV_HEADS):
                kcols = slice(kh * HEAD_DIM, (kh + 1) * HEAD_DIM)
                sink = sink_col(kh)
                q2 = stacked_q(sub, kh)
                s_ctx = _dot_nt(q2, k_ref[0:CTX_LEN, kcols]) * scale
                s_loc = jnp.where(ok, _dot_nt(q2, k_ref[pl.ds(off, nk), kcols]) * scale, MASK_VALUE)
                m = jnp.maximum(jnp.maximum(jnp.max(s_loc, axis=-1, keepdims=True),
                                            jnp.max(s_ctx, axis=-1, keepdims=True)), sink)
                p_loc = jnp.exp(s_loc - m)
                p_ctx = jnp.exp(s_ctx - m)
                l = jnp.sum(p_loc, axis=-1, keepdims=True) + jnp.sum(p_ctx, axis=-1, keepdims=True) + jnp.exp(sink - m)
                o = (_dot(p_loc.astype(bf16), v_ref[pl.ds(off, nk), kcols])
                     + _dot(p_ctx.astype(bf16), v_ref[0:CTX_LEN, kcols]))
                finish(sub, kh, o / l)


def _window_attention(qkv, sink, batch, t_per_batch):
    m = qkv.shape[0]
    tb = WA_BLOCK
    ts = tb * WA_STEP_BLOCKS
    nb = (t_per_batch - CTX_LEN) // tb
    assert nb * tb >= tb + 2 * WA_WINDOW and CTX_LEN % ts == 0 and t_per_batch % ts == 0
    nstep = t_per_batch // ts
    qcol = ROPE_LO // WA_WIDTH
    kcol = (ROPE_LO + WA_WIDTH) // WA_KV_WIDTH
    kv_spec = lambda col: pl.BlockSpec((t_per_batch, WA_KV_WIDTH), lambda b, j: (b, col), pipeline_mode=pl.Buffered(1))
    return pl.pallas_call(
        functools.partial(_wa_kernel, nb=nb),
        grid=(batch, nstep),
        in_specs=[
            pl.BlockSpec(memory_space=pltpu.SMEM),
            pl.BlockSpec((ts, WA_WIDTH), lambda b, j: (b * nstep + j, qcol)),
            kv_spec(kcol),
            kv_spec(kcol + 1),
        ],
        out_specs=pl.BlockSpec((ts, WA_WIDTH), lambda b, j: (b * nstep + j, 0)),
        out_shape=jax.ShapeDtypeStruct((m, WA_WIDTH), bf16),
        compiler_params=_cparams(("arbitrary", "arbitrary")),
        name="window_attention",
    )(sink.astype(f32), qkv, qkv, qkv)


def _conv_kernel(xp_ref, xc_ref, xn_ref, w_ref, b_ref, xs_ref, bc_ref, *, nblk):
    j = pl.program_id(0) % nblk
    tm = xc_ref.shape[0]
    prev_ok = jnp.where(j >= 2, 1.0, 0.0)
    next_ok = jnp.where((j >= 1) & (j <= nblk - 2), 1.0, 0.0)
    ext = jnp.concatenate([xp_ref[...] * prev_ok, xc_ref[...], xn_ref[...] * next_ok], axis=0)
    n_ext = tm + 2 * CONV_HALO
    acc = jnp.zeros((tm, SSD_XBC_WIDTH), f32) + b_ref[...]
    for k in range(SSD_CONV):
        shifted = pltpu.roll(ext, (SSD_CONV // 2 - k) % n_ext, 0)[CONV_HALO:CONV_HALO + tm]
        acc = acc + shifted * w_ref[k:k + 1, :]
    y = _silu(acc)
    xs_ref[...] = y[:, 0:SSD_D_INNER]
    bc_ref[...] = y[:, SSD_D_INNER:].astype(bf16)


def _ssd_conv(xbc, conv_w, conv_b, t_per_batch):
    m = xbc.shape[0]
    tm = ROW_BLOCK
    nblk = t_per_batch // tm
    hb = tm // CONV_HALO
    last = m // CONV_HALO - 1
    return pl.pallas_call(
        functools.partial(_conv_kernel, nblk=nblk),
        grid=(m // tm,),
        in_specs=[
            pl.BlockSpec((CONV_HALO, SSD_XBC_WIDTH), lambda i: (jnp.maximum(i * hb - 1, 0), 0)),
            pl.BlockSpec((tm, SSD_XBC_WIDTH), lambda i: (i, 0)),
            pl.BlockSpec((CONV_HALO, SSD_XBC_WIDTH), lambda i: (jnp.minimum((i + 1) * hb, last), 0)),
            pl.BlockSpec((SSD_CONV, SSD_XBC_WIDTH), lambda i: (0, 0)),
            pl.BlockSpec((1, SSD_XBC_WIDTH), lambda i: (0, 0)),
        ],
        out_specs=[
            pl.BlockSpec((tm, SSD_D_INNER), lambda i: (i, 0)),
            pl.BlockSpec((tm, SSD_BC_WIDTH), lambda i: (i, 0)),
        ],
        out_shape=[
            jax.ShapeDtypeStruct((m, SSD_D_INNER), f32),
            jax.ShapeDtypeStruct((m, SSD_BC_WIDTH), bf16),
        ],
        compiler_params=_cparams(("arbitrary",)),
        name="ssd_conv",
    )(xbc, xbc, xbc, conv_w, conv_b.reshape(1, -1))


def _expand_heads(small, e_ref):
    hi = small.astype(bf16)
    lo = (small - hi.astype(f32)).astype(bf16)
    e = e_ref[...]
    return _dot(hi, e) + _dot(lo, e)


def _ssd_direction(fwd, xs_ref, bc_ref, dt_ref, dtT_ref, alog_ref, alogc_ref, dtb_ref, dtbc_ref, dsk_ref, e_ref,
                   y_ref, state_ref):
    di = 0 if fwd else 1
    lc = SSD_CHUNK
    nh = SSD_HEADS
    gw = SSD_GROUP_WIDTH
    hs = slice(di * nh, (di + 1) * nh)
    dt = _softplus(dt_ref[:, hs] + dtb_ref[:, hs])
    dtT = _softplus(dtT_ref[hs, :] + dtbc_ref[hs, :])
    a_row = -jnp.exp(alog_ref[di:di + 1, :])
    a_col = -jnp.exp(alogc_ref[di])
    d_row = dsk_ref[di:di + 1, :]
    da = dt * a_row
    daT = dtT * a_col
    ri = lax.broadcasted_iota(i32, (lc, lc), 0)
    ci = lax.broadcasted_iota(i32, (lc, lc), 1)
    tri = (ri >= ci) if fwd else (ri <= ci)
    trif = tri.astype(f32)
    acs = jnp.dot(trif, da, preferred_element_type=f32, precision=lax.Precision.HIGHEST)
    acsT = lax.dot_general(daT, trif, (((1,), (1,)), ((), ())), preferred_element_type=f32,
                           precision=lax.Precision.HIGHEST)
    tot = jnp.sum(da, axis=0, keepdims=True)
    small = jnp.concatenate([jnp.exp(acs), jnp.exp(tot - acs) * dt,
                             jnp.broadcast_to(jnp.exp(tot), (8, nh)), jnp.broadcast_to(d_row, (8, nh))], axis=0)
    big = _expand_heads(small, e_ref)
    eacs_x = big[0:lc]
    w_x = big[lc:2 * lc]
    dec_x = big[2 * lc:2 * lc + 1]
    dsk_x = big[2 * lc + 8:2 * lc + 9]

    x = xs_ref[...]
    xw = (x * w_x).astype(bf16)
    xb = x.astype(bf16)
    for g in range(SSD_GROUPS):
        lo = g * gw
        bg = bc_ref[:, g * SSD_D_STATE:(g + 1) * SSD_D_STATE]
        cg = bc_ref[:, (SSD_GROUPS + g) * SSD_D_STATE:(SSD_GROUPS + g + 1) * SSD_D_STATE]
        cb = _dot_nt(cg, bg)
        st = state_ref[di, g]
        y_inter = _dot(cg, st.astype(bf16))
        state_ref[di, g] = st * dec_x[:, lo:lo + gw] + _dot_tn(bg, xw[:, lo:lo + gw])
        ys = []
        for k in range(SSD_HEADS_PER_GROUP):
            h = g * SSD_HEADS_PER_GROUP + k
            seg = jnp.exp(jnp.where(tri, acs[:, h:h + 1] - acsT[h:h + 1, :], -jnp.inf))
            mat = (cb * seg * dtT[h:h + 1, :]).astype(bf16)
            ys.append(_dot(mat, xb[:, h * SSD_HEAD_DIM:(h + 1) * SSD_HEAD_DIM]))
        y_g = jnp.concatenate(ys, axis=-1) + y_inter * eacs_x[:, lo:lo + gw] + x[:, lo:lo + gw] * dsk_x[:, lo:lo + gw]
        y_ref[:, lo:lo + gw] = y_g.astype(y_ref.dtype)


def _ssd_kernel(xsf_ref, bcf_ref, dtf_ref, dtTf_ref, xsb_ref, bcb_ref, dtb_ref, dtTb_ref,
                alog_ref, alogc_ref, bias_ref, biasc_ref, dsk_ref, e_ref, yf_ref, yb_ref, state_ref):
    @pl.when(pl.program_id(0) == 0)
    def _():
        state_ref[...] = jnp.zeros_like(state_ref)

    params = (alog_ref, alogc_ref, bias_ref, biasc_ref, dsk_ref, e_ref)
    for b in range(state_ref.shape[0]):
        _ssd_direction(True, xsf_ref.at[b], bcf_ref.at[b], dtf_ref.at[b], dtTf_ref.at[b], *params, yf_ref.at[b],
                       state_ref.at[b])
        _ssd_direction(False, xsb_ref.at[b], bcb_ref.at[b], dtb_ref.at[b], dtTb_ref.at[b], *params, yb_ref.at[b],
                       state_ref.at[b])


def _ssd_scan(xs, bc, dt_raw, dtT_raw, a_log, dt_bias, d_skip, batch, t_per_batch):
    m = xs.shape[0]
    lc = SSD_CHUNK
    nch = t_per_batch // lc
    nctx = CTX_LEN // lc

    def fwd_chunk(j):
        return j

    def bwd_chunk(j):
        return jnp.where(j < nctx, nctx - 1 - j, nch - 1 + nctx - j)

    def streams(chunk):
        return [
            pl.BlockSpec((batch, lc, SSD_D_INNER), lambda j: (0, chunk(j), 0)),
            pl.BlockSpec((batch, lc, SSD_BC_WIDTH), lambda j: (0, chunk(j), 0)),
            pl.BlockSpec((batch, lc, DT_WIDTH), lambda j: (0, chunk(j), 0)),
            pl.BlockSpec((batch, DT_WIDTH, lc), lambda j: (0, 0, chunk(j))),
        ]

    by_batch = lambda a: a.reshape(batch, t_per_batch, a.shape[-1])
    data = (by_batch(xs), by_batch(bc), by_batch(dt_raw), dtT_raw)
    expand = (jnp.arange(SSD_D_INNER)[None, :] // SSD_HEAD_DIM == jnp.arange(SSD_HEADS)[:, None]).astype(bf16)
    const2 = lambda j: (0, 0)
    y_f, y_b = pl.pallas_call(
        _ssd_kernel,
        grid=(nch,),
        in_specs=streams(fwd_chunk) + streams(bwd_chunk) + [
            pl.BlockSpec((2, SSD_HEADS), const2),
            pl.BlockSpec((2, SSD_HEADS, 1), lambda j: (0, 0, 0)),
            pl.BlockSpec((1, DT_WIDTH), const2),
            pl.BlockSpec((DT_WIDTH, 1), const2),
            pl.BlockSpec((2, SSD_HEADS), const2),
            pl.BlockSpec((SSD_HEADS, SSD_D_INNER), const2),
        ],
        out_specs=[
            pl.BlockSpec((batch, lc, SSD_D_INNER), lambda j: (0, fwd_chunk(j), 0)),
            pl.BlockSpec((batch, lc, SSD_D_INNER), lambda j: (0, bwd_chunk(j), 0)),
        ],
        out_shape=[jax.ShapeDtypeStruct((batch, t_per_batch, SSD_D_INNER), bf16)] * 2,
        scratch_shapes=[pltpu.VMEM((batch, 2, SSD_GROUPS, SSD_D_STATE, SSD_GROUP_WIDTH), f32)],
        compiler_params=_cparams(("arbitrary",)),
        name="ssd_scan",
    )(*data, *data,
      a_log.astype(f32), a_log.astype(f32).reshape(2, SSD_HEADS, 1),
      dt_bias.astype(f32).reshape(1, DT_WIDTH), dt_bias.astype(f32).reshape(DT_WIDTH, 1), d_skip.astype(f32), expand)
    return y_f.reshape(m, SSD_D_INNER), y_b.reshape(m, SSD_D_INNER)


def _outproj_kernel(oa_ref, ob_ref, yf_ref, yb_ref, z_ref, sgain_ref, w_ref, x_ref, mod_ref, ngain_ref, wrT_ref, rb_ref,
                    xmid_ref, hf_ref, e_ref, gate_ref, rank_ref, cnt_ref, carry_ref):
    d = D_MODEL
    u = (yf_ref[...].astype(f32) + yb_ref[...].astype(f32)) * _silu(z_ref[...].astype(f32))
    parts = []
    for g in range(SSD_GROUPS):
        ug = u[:, g * SSD_GROUP_WIDTH:(g + 1) * SSD_GROUP_WIDTH]
        parts.append(ug * lax.rsqrt(jnp.mean(ug * ug, axis=-1, keepdims=True) + EPS))
    gn = (jnp.concatenate(parts, axis=-1) * sgain_ref[...]).astype(bf16)
    acc = (_dot(oa_ref[...], w_ref[0:NA_WIDTH, :]) + _dot(ob_ref[...], w_ref[NA_WIDTH:NA_WIDTH + WA_WIDTH, :])
           + _dot(gn, w_ref[NA_WIDTH + WA_WIDTH:, :]))
    mod = mod_ref[0]
    xm = x_ref[...] + mod[:, 2 * d:3 * d] * acc
    xmid_ref[...] = xm
    hf = _rms_mod(xm, ngain_ref[...], mod[:, 3 * d:4 * d], mod[:, 4 * d:5 * d])
    hf_ref[...] = hf
    _route(hf, wrT_ref, rb_ref, e_ref, gate_ref, rank_ref, cnt_ref, carry_ref)


def _out_projection(o_a, o_b, y_f, y_b, z, ssd_norm, w_out, xa, mod4, ngain, w_router, router_bias, t_per_batch):
    m, d = xa.shape
    tm = ROW_BLOCK
    nblk = t_per_batch // tm
    row = lambda i: (i, 0)
    const = lambda i: (0, 0)
    mod_row = lambda i: (jnp.where(i % nblk == 0, 2, i // nblk), 0, 0)
    return pl.pallas_call(
        _outproj_kernel,
        grid=(m // tm,),
        in_specs=[
            pl.BlockSpec((tm, NA_WIDTH), row),
            pl.BlockSpec((tm, WA_WIDTH), row),
            pl.BlockSpec((tm, SSD_D_INNER), row),
            pl.BlockSpec((tm, SSD_D_INNER), row),
            pl.BlockSpec((tm, SSD_D_INNER), row),
            _resident((1, SSD_D_INNER), const),
            _resident((d, d), const),
            pl.BlockSpec((tm, d), row),
            pl.BlockSpec((1, 1, 6 * d), mod_row),
            _resident((1, d), const),
            _resident((N_EXPERTS, d), const),
            _resident((N_EXPERTS, 1), const),
        ],
        out_specs=[
            pl.BlockSpec((tm, d), row),
            pl.BlockSpec((tm, d), row),
            pl.BlockSpec((2, tm), lambda i: (0, i)),
            pl.BlockSpec((2, tm), lambda i: (0, i)),
            pl.BlockSpec((2, tm), lambda i: (0, i)),
            pl.BlockSpec((N_EXPERTS, 128), const),
        ],
        out_shape=[
            jax.ShapeDtypeStruct((m, d), f32),
            jax.ShapeDtypeStruct((m, d), f32),
            jax.ShapeDtypeStruct((2, m), i32),
            jax.ShapeDtypeStruct((2, m), f32),
            jax.ShapeDtypeStruct((2, m), i32),
            jax.ShapeDtypeStruct((N_EXPERTS, 128), i32),
        ],
        scratch_shapes=[pltpu.VMEM((N_EXPERTS, 1), f32)],
        compiler_params=_cparams(("arbitrary",)),
        name="out_projection",
    )(o_a, o_b, y_f, y_b, z, ssd_norm.reshape(1, -1), w_out.astype(bf16), xa, mod4, ngain.reshape(1, d),
      w_router.T.astype(bf16), router_bias.astype(f32).reshape(N_EXPERTS, 1))


def _route(hf, wrT_ref, rb_ref, e_ref, gate_ref, rank_ref, cnt_ref, carry_ref):
    i = pl.program_id(0)
    tm = hf.shape[0]
    ng, ge = N_EXPERT_GROUPS, EXPERTS_PER_GROUP

    @pl.when(i == 0)
    def _():
        carry_ref[...] = jnp.zeros_like(carry_ref)

    aff = _sigmoid(_dot_nt(wrT_ref[...], hf.astype(bf16)))
    sel3 = (aff + rb_ref[...]).reshape(ng, ge, tm)
    aff3 = aff.reshape(ng, ge, tm)
    io = lax.broadcasted_iota(i32, (ng, ge, tm), 1)
    m1 = jnp.max(sel3, axis=1, keepdims=True)
    i1 = jnp.min(jnp.where(sel3 == m1, io, ge), axis=1, keepdims=True)
    rest = jnp.where(io == i1, -jnp.inf, sel3)
    m2 = jnp.max(rest, axis=1, keepdims=True)
    i2 = jnp.min(jnp.where(rest == m2, io, ge), axis=1, keepdims=True)
    a1 = jnp.sum(jnp.where(io == i1, aff3, 0.0), axis=1)
    a2 = jnp.sum(jnp.where(io == i2, aff3, 0.0), axis=1)
    score = (m1 + m2)[:, 0, :]
    gi = lax.broadcasted_iota(i32, (ng, tm), 0)
    best = jnp.max(score, axis=0, keepdims=True)
    gb = jnp.min(jnp.where(score == best, gi, ng), axis=0, keepdims=True)
    picked = gi == gb
    pick_i = lambda a: jnp.sum(jnp.where(picked, a, 0), axis=0, keepdims=True)
    pick_f = lambda a: jnp.sum(jnp.where(picked, a, 0.0), axis=0, keepdims=True)
    e1 = gb * ge + pick_i(i1[:, 0, :])
    e2 = gb * ge + pick_i(i2[:, 0, :])
    g1 = pick_f(a1)
    g2 = pick_f(a2)
    den = g1 + g2
    e_ref[...] = jnp.concatenate([e1, e2], axis=0)
    gate_ref[...] = jnp.concatenate([g1 / den, g2 / den], axis=0)

    ei = lax.broadcasted_iota(i32, (N_EXPERTS, tm), 0)
    o1 = jnp.where(ei == e1, 1.0, 0.0)
    o2 = jnp.where(ei == e2, 1.0, 0.0)
    both = o1 + o2
    upper = jnp.where(lax.broadcasted_iota(i32, (tm, tm), 0) <= lax.broadcasted_iota(i32, (tm, tm), 1), 1.0, 0.0)
    incl = _dot(both.astype(bf16), upper.astype(bf16))
    before = incl - both + carry_ref[...]
    r1 = jnp.sum(o1 * before, axis=0, keepdims=True)
    r2 = jnp.sum(o2 * before, axis=0, keepdims=True)
    rank_ref[...] = jnp.concatenate([r1, r2], axis=0).astype(i32)
    total = carry_ref[...] + jnp.sum(both, axis=1, keepdims=True)
    carry_ref[...] = total
    cnt_ref[...] = jnp.broadcast_to(total, cnt_ref.shape).astype(i32)


def _dispatch_kernel(last_block_ref, tail_ref, slot_ref, hf_ref, xbuf_ref, zero_ref, sem, zero_sem):
    tm = slot_ref.shape[1]

    @pl.when(pl.program_id(0) == 0)
    def _():
        zero_ref[...] = jnp.zeros_like(zero_ref)

        def block_copy(b):
            r0 = pl.multiple_of(b * MOE_BLOCK, MOE_BLOCK)
            return pltpu.make_async_copy(zero_ref, xbuf_ref.at[pl.ds(r0, MOE_BLOCK)], zero_sem)

        def per_expert(fn):
            def body(e, carry):
                @pl.when(last_block_ref[e] >= 0)
                def _():
                    fn(last_block_ref[e])
                return carry
            lax.fori_loop(0, N_EXPERTS, body, 0)

        def per_tail(fn):
            def body(b, carry):
                fn(b)
                return carry
            lax.fori_loop(tail_ref[0], tail_ref[1], body, 0)

        per_expert(lambda b: block_copy(b).start())
        per_tail(lambda b: block_copy(b).start())
        per_expert(lambda b: block_copy(0).wait())
        per_tail(lambda b: block_copy(0).wait())

    def row_copy(r, s):
        return pltpu.make_async_copy(hf_ref.at[pl.ds(r, 1)], xbuf_ref.at[pl.ds(s, 1)], sem)

    def issue(q, carry):
        for u in range(ROW_UNROLL):
            r = q * ROW_UNROLL + u
            for k in range(2):
                row_copy(r, slot_ref[k, r]).start()
        return carry

    lax.fori_loop(0, tm // ROW_UNROLL, issue, 0)

    def drain(q, carry):
        for _ in range(2 * ROW_UNROLL):
            row_copy(0, 0).wait()
        return carry

    lax.fori_loop(0, tm // ROW_UNROLL, drain, 0)


def _dispatch(hf, slot, last_block, tail, n_pad):
    m, d = hf.shape
    tm = ROW_BLOCK
    return pl.pallas_call(
        _dispatch_kernel,
        grid_spec=pltpu.PrefetchScalarGridSpec(
            num_scalar_prefetch=2,
            grid=(m // tm,),
            in_specs=[
                pl.BlockSpec((2, tm), lambda i, lb, tl: (0, i), memory_space=pltpu.SMEM),
                pl.BlockSpec((tm, d), lambda i, lb, tl: (i, 0)),
            ],
            out_specs=pl.BlockSpec(memory_space=pl.ANY),
            scratch_shapes=[pltpu.VMEM((MOE_BLOCK, d), f32), pltpu.SemaphoreType.DMA, pltpu.SemaphoreType.DMA],
        ),
        out_shape=jax.ShapeDtypeStruct((n_pad, d), f32),
        compiler_params=_cparams(("arbitrary",)),
        name="moe_dispatch",
    )(last_block, tail, slot, hf)


def _cast_and_dot(x, w_ref, wb_ref):
    k = wb_ref.shape[0]
    ck = MOE_CAST_ROWS
    acc = None
    for r0 in range(0, k, ck):
        wc = w_ref[r0:r0 + ck, :].astype(bf16)
        wb_ref[r0:r0 + ck, :] = wc
        part = _dot(x[:, r0:r0 + ck], wc)
        acc = part if acc is None else acc + part
    return acc


def _moe_kernel(be_ref, nb_ref, next_ref, x_ref, wg_hbm, wu_hbm, wd_hbm, y_ref,
                wg_st, wu_st, wd_st, wgb_ref, wub_ref, wdb_ref, sems, *, layer):
    i = pl.program_id(0)
    used = i < nb_ref[0]
    fresh = jnp.logical_and(used, jnp.logical_or(i == 0, be_ref[i] != be_ref[jnp.maximum(i - 1, 0)]))
    staged = ((wg_hbm, wg_st), (wu_hbm, wu_st), (wd_hbm, wd_st))

    def stage(j, expert):
        src, dst = staged[j]
        return pltpu.make_async_copy(src.at[layer, expert], dst, sems.at[j])

    @pl.when(i == 0)
    def _():
        for j in range(len(staged)):
            stage(j, be_ref[0]).start()

    @pl.when(fresh)
    def _():
        x = x_ref[...].astype(bf16)
        stage(0, 0).wait()
        g = _cast_and_dot(x, wg_st, wgb_ref)
        stage(1, 0).wait()
        u = _cast_and_dot(x, wu_st, wub_ref)
        h = (_silu(g) * u).astype(bf16)
        stage(2, 0).wait()
        y_ref[...] = _cast_and_dot(h, wd_st, wdb_ref)

        @pl.when(next_ref[i] >= 0)
        def _():
            for j in range(len(staged)):
                stage(j, next_ref[i]).start()

    @pl.when(jnp.logical_and(used, jnp.logical_not(fresh)))
    def _():
        x = x_ref[...].astype(bf16)
        h = (_silu(_dot(x, wgb_ref[...])) * _dot(x, wub_ref[...])).astype(bf16)
        y_ref[...] = _dot(h, wdb_ref[...])

    @pl.when(jnp.logical_not(used))
    def _():
        y_ref[...] = jnp.zeros_like(y_ref)


def _moe_experts(xbuf, block_expert, n_blocks_used, next_expert, wg, wu, wd, layer):
    n_pad, d = xbuf.shape
    tme = MOE_BLOCK
    f = wg.shape[-1]
    hbm = pl.BlockSpec(memory_space=pl.ANY)
    return pl.pallas_call(
        functools.partial(_moe_kernel, layer=layer),
        grid_spec=pltpu.PrefetchScalarGridSpec(
            num_scalar_prefetch=3,
            grid=(n_pad // tme,),
            in_specs=[
                pl.BlockSpec((tme, d), lambda i, be, nb, nx: (jnp.minimum(i, nb[0] - 1), 0)),
                hbm, hbm, hbm,
            ],
            out_specs=pl.BlockSpec((tme, d), lambda i, be, nb, nx: (i, 0)),
            scratch_shapes=[
                pltpu.VMEM((d, f), f32), pltpu.VMEM((d, f), f32), pltpu.VMEM((f, d), f32),
                pltpu.VMEM((d, f), bf16), pltpu.VMEM((d, f), bf16), pltpu.VMEM((f, d), bf16),
                pltpu.SemaphoreType.DMA((3,)),
            ],
        ),
        out_shape=jax.ShapeDtypeStruct((n_pad, d), f32),
        compiler_params=_cparams(("arbitrary",)),
        name="moe_experts",
    )(block_expert, n_blocks_used, next_expert, xbuf, wg, wu, wd)


def _combine_kernel(slot_ref, slot_next_ref, ybuf_ref, gate_ref, x_ref, mod_ref, fgain_ref, o_ref, buf_ref, sems, *,
                    final_norm, n_steps):
    s = pl.program_id(0)
    tm = x_ref.shape[0]
    d = D_MODEL
    cur = s % 2

    def row_copy(b, k, r, src):
        return pltpu.make_async_copy(ybuf_ref.at[pl.ds(src, 1)], buf_ref.at[b, k, pl.ds(r, 1)], sems.at[b])

    def gather(rows_ref, b):
        def issue(q, carry):
            for u in range(ROW_UNROLL):
                r = q * ROW_UNROLL + u
                for k in range(2):
                    row_copy(b, k, r, rows_ref[k, r]).start()
            return carry

        lax.fori_loop(0, tm // ROW_UNROLL, issue, 0)

    @pl.when(s == 0)
    def _():
        gather(slot_ref, 0)

    @pl.when(s + 1 < n_steps)
    def _():
        gather(slot_next_ref, 1 - cur)

    def drain(q, carry):
        for _ in range(ROW_UNROLL):
            for k in range(2):
                row_copy(cur, k, 0, 0).wait()
        return carry

    lax.fori_loop(0, tm // ROW_UNROLL, drain, 0)
    gate = gate_ref[...]
    y = buf_ref[cur, 0] * gate[:, 0:1] + buf_ref[cur, 1] * gate[:, 1:2]
    xn = x_ref[...] + mod_ref[0][:, 5 * d:6 * d] * y
    if final_norm:
        xn = (xn * lax.rsqrt(jnp.mean(xn * xn, axis=-1, keepdims=True) + EPS)) * fgain_ref[...]
    o_ref[...] = xn


def _combine(ybuf, slot, gate_t, xmid, mod4, fgain, batch, t_per_batch, latent_only):
    m, d = xmid.shape
    tm = ROW_BLOCK
    nblk = t_per_batch // tm
    if latent_only:
        per_batch = nblk - 1
        src = lambda s: (s // per_batch) * nblk + 1 + s % per_batch
        modr = lambda s: s // per_batch
    else:
        per_batch = nblk
        src = lambda s: s
        modr = lambda s: jnp.where(s % nblk == 0, 2, s // nblk)
    n_steps = batch * per_batch
    nxt = lambda s: src(jnp.minimum(s + 1, n_steps - 1))
    return pl.pallas_call(
        functools.partial(_combine_kernel, final_norm=latent_only, n_steps=n_steps),
        grid=(n_steps,),
        in_specs=[
            pl.BlockSpec((2, tm), lambda s: (0, src(s)), memory_space=pltpu.SMEM),
            pl.BlockSpec((2, tm), lambda s: (0, nxt(s)), memory_space=pltpu.SMEM),
            pl.BlockSpec(memory_space=pl.ANY),
            pl.BlockSpec((tm, 2), lambda s: (src(s), 0)),
            pl.BlockSpec((tm, d), lambda s: (src(s), 0)),
            pl.BlockSpec((1, 1, 6 * d), lambda s: (modr(s), 0, 0)),
            pl.BlockSpec((1, d), lambda s: (0, 0)),
        ],
        out_specs=pl.BlockSpec((tm, d), lambda s: (s, 0)),
        out_shape=jax.ShapeDtypeStruct((n_steps * tm, d), f32),
        scratch_shapes=[pltpu.VMEM((2, 2, tm, d), f32), pltpu.SemaphoreType.DMA((2,))],
        compiler_params=_cparams(("arbitrary",)),
        name="moe_combine",
    )(slot, slot, ybuf, gate_t, xmid, mod4, fgain.reshape(1, d))


def _moe_plan(expert, rank, counts, n_assign):
    tme = MOE_BLOCK
    n_blocks = -(-n_assign // tme) + N_EXPERTS
    blocks_per_expert = (counts + tme - 1) // tme
    block_end = jnp.cumsum(blocks_per_expert)
    row_start = (block_end - blocks_per_expert) * tme
    chosen = expert[..., None] == jnp.arange(N_EXPERTS)
    slot = rank + jnp.sum(jnp.where(chosen, row_start, 0), axis=-1)
    block_expert = jnp.minimum(jnp.sum(block_end[None, :] <= jnp.arange(n_blocks)[:, None], axis=1), N_EXPERTS - 1)
    last_block = jnp.where(blocks_per_expert > 0, block_end - 1, -1)
    tail = jnp.stack([block_end[-1], jnp.asarray(n_blocks, block_end.dtype)])
    ids = jnp.arange(N_EXPERTS)
    later = (ids[None, :] > ids[:, None]) & (blocks_per_expert > 0)[None, :]
    following = jnp.min(jnp.where(later, ids[None, :], N_EXPERTS), axis=1)
    following = jnp.where(following < N_EXPERTS, following, -1)
    onehot = block_expert[:, None] == ids[None, :]
    next_expert = jnp.sum(jnp.where(onehot, following[None, :], 0), axis=1)
    return (slot.astype(i32), block_expert.astype(i32), block_end[-1:].astype(i32), last_block.astype(i32),
            tail.astype(i32), next_expert.astype(i32), n_blocks * tme)


def kernel(x, c, ctx, c_ctx, w_ada, b_ada, norm_mix, norm_ffn, norm_final, w_in, w_out, na_rpb, wa_sink,
           ssd_conv_w, ssd_conv_b, ssd_a_log, ssd_dt_bias, ssd_d, ssd_norm, w_router, router_bias,
           w_gate, w_up, w_down):
    batch, seq, d = x.shape
    depth = w_ada.shape[0]
    assert d == D_MODEL and ctx.shape[1] == CTX_LEN and seq % ROW_BLOCK == 0 and batch <= 2
    t_per_batch = CTX_LEN + seq
    m = batch * t_per_batch
    rows = seq // GRID_W

    xa = jnp.concatenate([ctx, x], axis=1).reshape(m, d)
    cc = jnp.zeros((8, d), f32).at[0:batch].set(c).at[2].set(c_ctx)
    mod_all = _ada_modulation(cc, w_ada, b_ada).reshape(depth, 8, 1, 6 * d)
    rope_tabs = _rope_tables(seq)

    out = None
    for l in range(depth):
        mod4 = mod_all[l]
        qkv, z, xbc, dt_raw, dtT_raw = _in_projection(xa, mod4, norm_mix[l], w_in[l], rope_tabs, t_per_batch)
        o_a = _neighbourhood_attention(qkv, _na_bias_tables(na_rpb[l], rows), batch, t_per_batch)
        o_b = _window_attention(qkv, wa_sink[l], batch, t_per_batch)
        xs, bc = _ssd_conv(xbc, ssd_conv_w[l], ssd_conv_b[l], t_per_batch)
        y_f, y_b = _ssd_scan(xs, bc, dt_raw, dtT_raw, ssd_a_log[l], ssd_dt_bias[l], ssd_d[l], batch, t_per_batch)
        xmid, hf, expert, gate, rank, cnt = _out_projection(o_a, o_b, y_f, y_b, z, ssd_norm[l], w_out[l], xa, mod4,
                                                            norm_ffn[l], w_router, router_bias, t_per_batch)
        slot, block_expert, n_used, last_block, tail, next_expert, n_pad = _moe_plan(expert, rank, cnt[:, 0], 2 * m)
        xbuf = _dispatch(hf, slot, last_block, tail, n_pad)
        ybuf = _moe_experts(xbuf, block_expert, n_used, next_expert, w_gate, w_up, w_down, l)
        last = l == depth - 1
        res = _combine(ybuf, slot, gate.T, xmid, mod4, norm_final, batch, t_per_batch, latent_only=last)
        if last:
            out = res.reshape(batch, seq, d)
        else:
            xa = res
    return out
```

```python
import functools

import jax
import jax.numpy as jnp
import numpy as np
from jax import lax
from jax.experimental import pallas as pl
from jax.experimental.pallas import tpu as pltpu

f32 = jnp.float32
bf16 = jnp.bfloat16
i32 = jnp.int32

D_MODEL = 2048
GRID_W = 64
CTX_LEN = 256
EPS = 1e-6
MASK_VALUE = -1e30
HEAD_DIM = 128
NA_WIDTH = D_MODEL // 4
NA_HEADS = NA_WIDTH // HEAD_DIM
NA_WIN_R = 8
NA_WIN_C = 16
WA_WIDTH = D_MODEL // 4
WA_HEADS = WA_WIDTH // HEAD_DIM
WA_KV_HEADS = 2
WA_KV_WIDTH = WA_KV_HEADS * HEAD_DIM
WA_WINDOW = 128
WA_BLOCK = 128
ROPE_BASE = 10000.0
SSD_D_INNER = D_MODEL // 2
SSD_HEAD_DIM = 64
SSD_HEADS = SSD_D_INNER // SSD_HEAD_DIM
SSD_GROUPS = 2
SSD_D_STATE = 128
SSD_CONV = 5
SSD_BC_WIDTH = 2 * SSD_GROUPS * SSD_D_STATE
SSD_XBC_WIDTH = SSD_D_INNER + SSD_BC_WIDTH
SSD_GROUP_WIDTH = SSD_D_INNER // SSD_GROUPS
SSD_HEADS_PER_GROUP = SSD_HEADS // SSD_GROUPS
N_EXPERTS = 32
N_EXPERT_GROUPS = 4
EXPERTS_PER_GROUP = N_EXPERTS // N_EXPERT_GROUPS
D_FF_EXPERT = D_MODEL // 2

QKV_WIDTH = 3 * NA_WIDTH + WA_WIDTH + 2 * WA_KV_WIDTH
ROPE_LO = 3 * NA_WIDTH
ROPE_HI = ROPE_LO + WA_WIDTH + WA_KV_WIDTH
Z_LO = QKV_WIDTH
XBC_LO = Z_LO + SSD_D_INNER
DT_LO = XBC_LO + SSD_XBC_WIDTH
DT_WIDTH = 2 * SSD_HEADS

ROW_BLOCK = 256
SSD_CHUNK = 128
NA_GROUP_ROWS = 4
WA_STEP_BLOCKS = 2
NA_KEY_ROWS = NA_GROUP_ROWS + NA_WIN_R
MOE_BLOCK = 256
MOE_CAST_ROWS = 256
ROW_UNROLL = 8
CONV_HALO = 8
VMEM_LIMIT = 56 * 1024 * 1024


def _cparams(sem):
    return pltpu.CompilerParams(dimension_semantics=sem, vmem_limit_bytes=VMEM_LIMIT)


def _sigmoid(x):
    return 1.0 / (1.0 + jnp.exp(-x))


def _silu(x):
    return x * _sigmoid(x)


def _softplus(x):
    return jnp.maximum(x, 0.0) + jnp.log1p(jnp.exp(-jnp.abs(x)))


def _dot(a, b):
    return jnp.dot(a, b, preferred_element_type=f32)


def _dot_nt(a, b):
    return lax.dot_general(a, b, (((1,), (1,)), ((), ())), preferred_element_type=f32)


def _dot_tn(a, b):
    return lax.dot_general(a, b, (((0,), (0,)), ((), ())), preferred_element_type=f32)


def _resident(shape, index_map):
    return pl.BlockSpec(shape, index_map, pipeline_mode=pl.Buffered(1))


def _ada_kernel(c_ref, w_ref, b_ref, o_ref):
    s = _silu(c_ref[...]).astype(bf16)
    o_ref[0] = _dot(s, w_ref[0].astype(bf16)) + b_ref[0]


def _ada_modulation(cc, w_ada, b_ada):
    depth, d, n = w_ada.shape
    tn = 1024
    return pl.pallas_call(
        _ada_kernel,
        grid=(depth, n // tn),
        in_specs=[
            pl.BlockSpec((8, d), lambda l, j: (0, 0)),
            pl.BlockSpec((1, d, tn), lambda l, j: (l, 0, j)),
            pl.BlockSpec((1, 1, tn), lambda l, j: (l, 0, j)),
        ],
        out_specs=pl.BlockSpec((1, 8, tn), lambda l, j: (l, 0, j)),
        out_shape=jax.ShapeDtypeStruct((depth, 8, n), f32),
        compiler_params=_cparams(("arbitrary", "arbitrary")),
        name="ada_modulation",
    )(cc, w_ada, b_ada.reshape(depth, 1, n))


def _rms_mod(x, gain, shift, scale):
    y = x * lax.rsqrt(jnp.mean(x * x, axis=-1, keepdims=True) + EPS)
    return (y * gain) * (1.0 + scale) + shift


def _inproj_kernel(x_ref, mod_ref, gain_ref, wqkv_ref, wz_ref, wxbc_ref, wdt_ref, wdtT_ref,
                   cos_ref, sina_ref, sinb_ref, qkv_ref, z_ref, xbc_ref, dt_ref, dtT_ref):
    d = D_MODEL
    mod = mod_ref[0]
    h = _rms_mod(x_ref[...], gain_ref[...], mod[:, 0:d], mod[:, d:2 * d]).astype(bf16)
    acc = _dot(h, wqkv_ref[...])
    qkv_ref[:, 0:ROPE_LO] = acc[:, 0:ROPE_LO].astype(bf16)
    cos, sina, sinb = cos_ref[...], sina_ref[...], sinb_ref[...]
    quarter = HEAD_DIM // 4
    for c0 in range(ROPE_LO, ROPE_HI, HEAD_DIM):
        xh = acc[:, c0:c0 + HEAD_DIM]
        rot = (xh * cos + pltpu.roll(xh, HEAD_DIM - quarter, 1) * sina + pltpu.roll(xh, quarter, 1) * sinb)
        qkv_ref[:, c0:c0 + HEAD_DIM] = rot.astype(bf16)
    qkv_ref[:, ROPE_HI:QKV_WIDTH] = acc[:, ROPE_HI:QKV_WIDTH].astype(bf16)
    z_ref[...] = _dot(h, wz_ref[...]).astype(z_ref.dtype)
    xbc_ref[...] = _dot(h, wxbc_ref[...])
    dt_ref[...] = _dot(h, wdt_ref[...])
    dtT_ref[0] = _dot_nt(wdtT_ref[...], h)


def _in_projection(xa, mod4, gain, w_in, rope_tabs, t_per_batch):
    m, d = xa.shape
    tm = ROW_BLOCK
    nblk = t_per_batch // tm
    wb = w_in.astype(bf16)
    wqkv = wb[:, 0:QKV_WIDTH]
    wz = wb[:, Z_LO:XBC_LO]
    wxbc = wb[:, XBC_LO:DT_LO]
    wdt = wb[:, DT_LO:DT_LO + DT_WIDTH]
    wdtT = wdt.T
    cos, sina, sinb = rope_tabs
    row = lambda i: (i, 0)
    const = lambda i: (0, 0)
    mod_row = lambda i: (jnp.where(i % nblk == 0, 2, i // nblk), 0, 0)
    tab_row = lambda i: (i % nblk, 0)
    return pl.pallas_call(
        _inproj_kernel,
        grid=(m // tm,),
        in_specs=[
            pl.BlockSpec((tm, d), row),
            pl.BlockSpec((1, 1, 6 * d), mod_row),
            _resident((1, d), const),
            _resident((d, QKV_WIDTH), const),
            _resident((d, SSD_D_INNER), const),
            _resident((d, SSD_XBC_WIDTH), const),
            _resident((d, DT_WIDTH), const),
            _resident((DT_WIDTH, d), const),
            pl.BlockSpec((tm, HEAD_DIM), tab_row),
            pl.BlockSpec((tm, HEAD_DIM), tab_row),
            pl.BlockSpec((tm, HEAD_DIM), tab_row),
        ],
        out_specs=[
            pl.BlockSpec((tm, QKV_WIDTH), row),
            pl.BlockSpec((tm, SSD_D_INNER), row),
            pl.BlockSpec((tm, SSD_XBC_WIDTH), row),
            pl.BlockSpec((tm, DT_WIDTH), row),
            pl.BlockSpec((1, DT_WIDTH, tm), lambda i: (i // nblk, 0, i % nblk)),
        ],
        out_shape=[
            jax.ShapeDtypeStruct((m, QKV_WIDTH), bf16),
            jax.ShapeDtypeStruct((m, SSD_D_INNER), bf16),
            jax.ShapeDtypeStruct((m, SSD_XBC_WIDTH), f32),
            jax.ShapeDtypeStruct((m, DT_WIDTH), f32),
            jax.ShapeDtypeStruct((m // t_per_batch, DT_WIDTH, t_per_batch), f32),
        ],
        compiler_params=_cparams(("arbitrary",)),
        name="in_projection",
    )(xa, mod4, gain.reshape(1, d), wqkv, wz, wxbc, wdt, wdtT, cos, sina, sinb)


def _rope_tables(seq):
    assert seq % GRID_W == 0
    rows = seq // GRID_W
    q = HEAD_DIM // 4
    inv = jnp.power(ROPE_BASE, -jnp.arange(q, dtype=f32) / q)
    ang_r = jnp.arange(rows, dtype=f32)[:, None] * inv
    ang_c = jnp.arange(GRID_W, dtype=f32)[:, None] * inv
    by_row = lambda a: jnp.repeat(a, GRID_W, axis=0)
    by_col = lambda a: jnp.tile(a, (rows, 1))
    cos_r, sin_r = by_row(jnp.cos(ang_r)), by_row(jnp.sin(ang_r))
    cos_c, sin_c = by_col(jnp.cos(ang_c)), by_col(jnp.sin(ang_c))
    zero = jnp.zeros_like(cos_r)
    cos = jnp.concatenate([cos_r, cos_r, cos_c, cos_c], axis=-1)
    sina = jnp.concatenate([-sin_r, zero, -sin_c, zero], axis=-1)
    sinb = jnp.concatenate([zero, sin_r, zero, sin_c], axis=-1)
    pad = lambda a, v: jnp.concatenate([jnp.full((CTX_LEN, HEAD_DIM), v, f32), a], axis=0)
    return pad(cos, 1.0), pad(sina, 0.0), pad(sinb, 0.0)


def _na_bias_tables(rpb, rows):
    col = np.arange(GRID_W)
    c0 = np.clip(col - NA_WIN_C // 2, 0, GRID_W - NA_WIN_C)
    col_ok = (col[None, :] >= c0[:, None]) & (col[None, :] < c0[:, None] + NA_WIN_C)
    dc = np.clip(col[None, :] - col[:, None] + NA_WIN_C - 1, 0, 2 * NA_WIN_C - 2)
    dc_hot = (dc[:, :, None] == np.arange(2 * NA_WIN_C - 1)).astype(np.float32)
    exact = lax.Precision.HIGHEST
    by_col = jnp.einsum("hde,qke->hdqk", rpb.astype(f32), dc_hot, precision=exact)
    tabs = []
    for r_first in (NA_GROUP_ROWS, 0, rows - NA_GROUP_ROWS):
        start = min(max(r_first - NA_WIN_R // 2, 0), rows - NA_KEY_ROWS)
        r = r_first + np.arange(NA_GROUP_ROWS)
        kr = start + np.arange(NA_KEY_ROWS)
        r0 = np.clip(r - NA_WIN_R // 2, 0, rows - NA_WIN_R)
        row_ok = (kr[None, :] >= r0[:, None]) & (kr[None, :] < r0[:, None] + NA_WIN_R)
        dr = np.clip(kr[None, :] - r[:, None] + NA_WIN_R - 1, 0, 2 * NA_WIN_R - 2)
        dr_hot = (dr[:, :, None] == np.arange(2 * NA_WIN_R - 1)).astype(np.float32)
        b = jnp.einsum("gid,hdqk->hgqik", dr_hot, by_col, precision=exact)
        ok = row_ok[:, None, :, None] & col_ok[None, :, None, :]
        b = jnp.where(ok[None], b, MASK_VALUE)
        tabs.append(b.reshape(rpb.shape[0], NA_GROUP_ROWS * GRID_W, NA_KEY_ROWS * GRID_W))
    return jnp.stack(tabs, axis=1)


def _na_phases(g, q_ref, k_ref, v_ref, bias_ref, o_ref, rows):
    scale = HEAD_DIM ** -0.5
    nk = NA_KEY_ROWS * GRID_W

    def head_cols(h):
        return slice(h * HEAD_DIM, (h + 1) * HEAD_DIM)

    def context():
        for h in range(NA_HEADS):
            hc = head_cols(h)
            s_ctx = _dot_nt(q_ref[:, hc], k_ref[0:CTX_LEN, hc]) * scale
            m = jnp.max(s_ctx, axis=-1, keepdims=True)
            p = jnp.exp(s_ctx - m)
            l = jnp.sum(p, axis=-1, keepdims=True)
            o_ref[:, hc] = (_dot(p.astype(bf16), v_ref[0:CTX_LEN, hc]) / l).astype(bf16)

    def latent():
        r_first = (g - 1) * NA_GROUP_ROWS
        start = jnp.clip(r_first - NA_WIN_R // 2, 0, rows - NA_KEY_ROWS)
        off = pl.multiple_of(CTX_LEN + start * GRID_W, GRID_W)
        for h in range(NA_HEADS):
            hc = head_cols(h)
            q = q_ref[:, hc]
            vc = v_ref[0:CTX_LEN, hc]
            s_ctx = _dot_nt(q, k_ref[0:CTX_LEN, hc]) * scale
            s_loc = _dot_nt(q, k_ref[pl.ds(off, nk), hc]) * scale + bias_ref[h, 0]
            m = jnp.maximum(jnp.max(s_loc, axis=-1, keepdims=True), jnp.max(s_ctx, axis=-1, keepdims=True))
            p_loc = jnp.exp(s_loc - m)
            p_ctx = jnp.exp(s_ctx - m)
            l = jnp.sum(p_loc, axis=-1, keepdims=True) + jnp.sum(p_ctx, axis=-1, keepdims=True)
            o = _dot(p_loc.astype(bf16), v_ref[pl.ds(off, nk), hc]) + _dot(p_ctx.astype(bf16), vc)
            o_ref[:, hc] = (o / l).astype(bf16)

    return context, latent


def _wa_phases(j, sink_ref, q_ref, k_ref, v_ref, o_ref, nb):
    scale = HEAD_DIM ** -0.5
    tb = WA_BLOCK
    g = WA_HEADS // WA_KV_HEADS
    nk = tb + 2 * WA_WINDOW
    ctx_steps = CTX_LEN // (tb * WA_STEP_BLOCKS)
    rowi = lax.broadcasted_iota(i32, (g * tb, 1), 0)

    def stacked_q(sub, kh):
        rows = slice(sub * tb, (sub + 1) * tb)
        return jnp.concatenate([q_ref[rows, (kh * g + i) * HEAD_DIM:(kh * g + i + 1) * HEAD_DIM] for i in range(g)], axis=0)

    def sink_col(kh):
        sink = jnp.full((g * tb, 1), sink_ref[kh * g], f32)
        for i in range(1, g):
            sink = jnp.where(rowi >= i * tb, sink_ref[kh * g + i], sink)
        return sink

    def finish(sub, kh, o):
        for i in range(g):
            o_ref[sub * tb:(sub + 1) * tb, (kh * g + i) * HEAD_DIM:(kh * g + i + 1) * HEAD_DIM] = (
                o[i * tb:(i + 1) * tb].astype(bf16))

    def context():
        for sub in range(WA_STEP_BLOCKS):
            for kh in range(WA_KV_HEADS):
                kcols = slice(kh * HEAD_DIM, (kh + 1) * HEAD_DIM)
                sink = sink_col(kh)
                s_ctx = _dot_nt(stacked_q(sub, kh), k_ref[0:CTX_LEN, kcols]) * scale
                m = jnp.maximum(jnp.max(s_ctx, axis=-1, keepdims=True), sink)
                p = jnp.exp(s_ctx - m)
                l = jnp.sum(p, axis=-1, keepdims=True) + jnp.exp(sink - m)
                finish(sub, kh, _dot(p.astype(bf16), v_ref[0:CTX_LEN, kcols]) / l)

    def latent():
        for sub in range(WA_STEP_BLOCKS):
            n = (j - ctx_steps) * WA_STEP_BLOCKS + sub
            st = jnp.clip(n * tb - WA_WINDOW, 0, nb * tb - nk)
            off = pl.multiple_of(CTX_LEN + st, tb)
            qpos = n * tb + lax.broadcasted_iota(i32, (g * tb, nk), 0) % tb
            kpos = st + lax.broadcasted_iota(i32, (g * tb, nk), 1)
            ok = jnp.abs(qpos - kpos) <= WA_WINDOW
            for kh in range(WA_KV_HEADS):
                kcols = slice(kh * HEAD_DIM, (kh + 1) * HEAD_DIM)
                sink = sink_col(kh)
                q2 = stacked_q(sub, kh)
                s_ctx = _dot_nt(q2, k_ref[0:CTX_LEN, kcols]) * scale
                s_loc = jnp.where(ok, _dot_nt(q2, k_ref[pl.ds(off, nk), kcols]) * scale, MASK_VALUE)
                m = jnp.maximum(jnp.maximum(jnp.max(s_loc, axis=-1, keepdims=True),
                                            jnp.max(s_ctx, axis=-1, keepdims=True)), sink)
                p_loc = jnp.exp(s_loc - m)
                p_ctx = jnp.exp(s_ctx - m)
                l = jnp.sum(p_loc, axis=-1, keepdims=True) + jnp.sum(p_ctx, axis=-1, keepdims=True) + jnp.exp(sink - m)
                o = (_dot(p_loc.astype(bf16), v_ref[pl.ds(off, nk), kcols])
                     + _dot(p_ctx.astype(bf16), v_ref[0:CTX_LEN, kcols]))
                finish(sub, kh, o / l)

    return context, latent


def _attention_kernel(sink_ref, qa_ref, ka_ref, va_ref, bias_ref, qb_ref, kb_ref, vb_ref, oa_ref, ob_ref, *, rows, nb):
    g = pl.program_id(1)
    na_context, na_latent = _na_phases(g, qa_ref, ka_ref, va_ref, bias_ref, oa_ref, rows)
    wa_context, wa_latent = _wa_phases(g, sink_ref, qb_ref, kb_ref, vb_ref, ob_ref, nb)

    @pl.when(g == 0)
    def _():
        na_context()
        wa_context()

    @pl.when(g > 0)
    def _():
        na_latent()
        wa_latent()


def _attention(qkv, bias_tabs, sink, batch, t_per_batch):
    m = qkv.shape[0]
    tq = ROW_BLOCK
    rows = (t_per_batch - CTX_LEN) // GRID_W
    nb = (t_per_batch - CTX_LEN) // WA_BLOCK
    assert tq == CTX_LEN == NA_GROUP_ROWS * GRID_W == WA_STEP_BLOCKS * WA_BLOCK
    assert rows % NA_GROUP_ROWS == 0 and rows >= NA_KEY_ROWS + NA_GROUP_ROWS and nb * WA_BLOCK >= WA_BLOCK + 2 * WA_WINDOW
    ng = rows // NA_GROUP_ROWS
    nblk = t_per_batch // tq
    bias_idx = lambda b, g: (0, jnp.where(g == 1, 1, jnp.where(g == ng, 2, 0)), 0, 0)
    q_spec = lambda width, col: pl.BlockSpec((tq, width), lambda b, g: (b * nblk + g, col))
    kv_spec = lambda width, col: pl.BlockSpec((t_per_batch, width), lambda b, g: (b, col), pipeline_mode=pl.Buffered(1))
    wa_q = ROPE_LO // WA_WIDTH
    wa_k = (ROPE_LO + WA_WIDTH) // WA_KV_WIDTH
    return pl.pallas_call(
        functools.partial(_attention_kernel, rows=rows, nb=nb),
        grid=(batch, nblk),
        in_specs=[
            pl.BlockSpec(memory_space=pltpu.SMEM),
            q_spec(NA_WIDTH, 0),
            kv_spec(NA_WIDTH, 1),
            kv_spec(NA_WIDTH, 2),
            pl.BlockSpec((NA_HEADS, 1, tq, NA_KEY_ROWS * GRID_W), bias_idx),
            q_spec(WA_WIDTH, wa_q),
            kv_spec(WA_KV_WIDTH, wa_k),
            kv_spec(WA_KV_WIDTH, wa_k + 1),
        ],
        out_specs=[q_spec(NA_WIDTH, 0), q_spec(WA_WIDTH, 0)],
        out_shape=[jax.ShapeDtypeStruct((m, NA_WIDTH), bf16), jax.ShapeDtypeStruct((m, WA_WIDTH), bf16)],
        compiler_params=_cparams(("arbitrary", "arbitrary")),
        name="attention",
    )(sink.astype(f32), qkv, qkv, qkv, bias_tabs, qkv, qkv, qkv)


def _conv_kernel(xp_ref, xc_ref, xn_ref, w_ref, b_ref, xs_ref, bc_ref, *, nblk):
    j = pl.program_id(0) % nblk
    tm = xc_ref.shape[0]
    prev_ok = jnp.where(j >= 2, 1.0, 0.0)
    next_ok = jnp.where((j >= 1) & (j <= nblk - 2), 1.0, 0.0)
    ext = jnp.concatenate([xp_ref[...] * prev_ok, xc_ref[...], xn_ref[...] * next_ok], axis=0)
    n_ext = tm + 2 * CONV_HALO
    acc = jnp.zeros((tm, SSD_XBC_WIDTH), f32) + b_ref[...]
    for k in range(SSD_CONV):
        shifted = pltpu.roll(ext, (SSD_CONV // 2 - k) % n_ext, 0)[CONV_HALO:CONV_HALO + tm]
        acc = acc + shifted * w_ref[k:k + 1, :]
    y = _silu(acc)
    xs_ref[...] = y[:, 0:SSD_D_INNER]
    bc_ref[...] = y[:, SSD_D_INNER:].astype(bf16)


def _ssd_conv(xbc, conv_w, conv_b, t_per_batch):
    m = xbc.shape[0]
    tm = ROW_BLOCK
    nblk = t_per_batch // tm
    hb = tm // CONV_HALO
    last = m // CONV_HALO - 1
    return pl.pallas_call(
        functools.partial(_conv_kernel, nblk=nblk),
        grid=(m // tm,),
        in_specs=[
            pl.BlockSpec((CONV_HALO, SSD_XBC_WIDTH), lambda i: (jnp.maximum(i * hb - 1, 0), 0)),
            pl.BlockSpec((tm, SSD_XBC_WIDTH), lambda i: (i, 0)),
            pl.BlockSpec((CONV_HALO, SSD_XBC_WIDTH), lambda i: (jnp.minimum((i + 1) * hb, last), 0)),
            pl.BlockSpec((SSD_CONV, SSD_XBC_WIDTH), lambda i: (0, 0)),
            pl.BlockSpec((1, SSD_XBC_WIDTH), lambda i: (0, 0)),
        ],
        out_specs=[
            pl.BlockSpec((tm, SSD_D_INNER), lambda i: (i, 0)),
            pl.BlockSpec((tm, SSD_BC_WIDTH), lambda i: (i, 0)),
        ],
        out_shape=[
            jax.ShapeDtypeStruct((m, SSD_D_INNER), f32),
            jax.ShapeDtypeStruct((m, SSD_BC_WIDTH), bf16),
        ],
        compiler_params=_cparams(("arbitrary",)),
        name="ssd_conv",
    )(xbc, xbc, xbc, conv_w, conv_b.reshape(1, -1))


def _expand_heads(small, e_ref):
    hi = small.astype(bf16)
    lo = (small - hi.astype(f32)).astype(bf16)
    e = e_ref[...]
    return _dot(hi, e) + _dot(lo, e)


def _ssd_direction(fwd, xs_ref, bc_ref, dt_ref, dtT_ref, alog_ref, alogc_ref, dtb_ref, dtbc_ref, dsk_ref, e_ref,
                   y_ref, state_ref):
    di = 0 if fwd else 1
    lc = SSD_CHUNK
    nh = SSD_HEADS
    gw = SSD_GROUP_WIDTH
    hs = slice(di * nh, (di + 1) * nh)
    dt = _softplus(dt_ref[:, hs] + dtb_ref[:, hs])
    dtT = _softplus(dtT_ref[hs, :] + dtbc_ref[hs, :])
    a_row = -jnp.exp(alog_ref[di:di + 1, :])
    a_col = -jnp.exp(alogc_ref[di])
    d_row = dsk_ref[di:di + 1, :]
    da = dt * a_row
    daT = dtT * a_col
    ri = lax.broadcasted_iota(i32, (lc, lc), 0)
    ci = lax.broadcasted_iota(i32, (lc, lc), 1)
    tri = (ri >= ci) if fwd else (ri <= ci)
    trif = tri.astype(f32)
    acs = jnp.dot(trif, da, preferred_element_type=f32, precision=lax.Precision.HIGHEST)
    acsT = lax.dot_general(daT, trif, (((1,), (1,)), ((), ())), preferred_element_type=f32,
                           precision=lax.Precision.HIGHEST)
    tot = jnp.sum(da, axis=0, keepdims=True)
    small = jnp.concatenate([jnp.exp(acs), jnp.exp(tot - acs) * dt,
                             jnp.broadcast_to(jnp.exp(tot), (8, nh)), jnp.broadcast_to(d_row, (8, nh))], axis=0)
    big = _expand_heads(small, e_ref)
    eacs_x = big[0:lc]
    w_x = big[lc:2 * lc]
    dec_x = big[2 * lc:2 * lc + 1]
    dsk_x = big[2 * lc + 8:2 * lc + 9]

    x = xs_ref[...]
    xw = (x * w_x).astype(bf16)
    xb = x.astype(bf16)
    for g in range(SSD_GROUPS):
        lo = g * gw
        bg = bc_ref[:, g * SSD_D_STATE:(g + 1) * SSD_D_STATE]
        cg = bc_ref[:, (SSD_GROUPS + g) * SSD_D_STATE:(SSD_GROUPS + g + 1) * SSD_D_STATE]
        cb = _dot_nt(cg, bg)
        st = state_ref[di, g]
        y_inter = _dot(cg, st.astype(bf16))
        state_ref[di, g] = st * dec_x[:, lo:lo + gw] + _dot_tn(bg, xw[:, lo:lo + gw])
        ys = []
        for k in range(SSD_HEADS_PER_GROUP):
            h = g * SSD_HEADS_PER_GROUP + k
            seg = jnp.exp(jnp.where(tri, acs[:, h:h + 1] - acsT[h:h + 1, :], -jnp.inf))
            mat = (cb * seg * dtT[h:h + 1, :]).astype(bf16)
            ys.append(_dot(mat, xb[:, h * SSD_HEAD_DIM:(h + 1) * SSD_HEAD_DIM]))
        y_g = jnp.concatenate(ys, axis=-1) + y_inter * eacs_x[:, lo:lo + gw] + x[:, lo:lo + gw] * dsk_x[:, lo:lo + gw]
        y_ref[:, lo:lo + gw] = y_g.astype(y_ref.dtype)


def _ssd_kernel(xsf_ref, bcf_ref, dtf_ref, dtTf_ref, xsb_ref, bcb_ref, dtb_ref, dtTb_ref,
                alog_ref, alogc_ref, bias_ref, biasc_ref, dsk_ref, e_ref, yf_ref, yb_ref, state_ref):
    @pl.when(pl.program_id(0) == 0)
    def _():
        state_ref[...] = jnp.zeros_like(state_ref)

    params = (alog_ref, alogc_ref, bias_ref, biasc_ref, dsk_ref, e_ref)
    for b in range(state_ref.shape[0]):
        _ssd_direction(True, xsf_ref.at[b], bcf_ref.at[b], dtf_ref.at[b], dtTf_ref.at[b], *params, yf_ref.at[b],
                       state_ref.at[b])
        _ssd_direction(False, xsb_ref.at[b], bcb_ref.at[b], dtb_ref.at[b], dtTb_ref.at[b], *params, yb_ref.at[b],
                       state_ref.at[b])


def _ssd_scan(xs, bc, dt_raw, dtT_raw, a_log, dt_bias, d_skip, batch, t_per_batch):
    m = xs.shape[0]
    lc = SSD_CHUNK
    nch = t_per_batch // lc
    nctx = CTX_LEN // lc

    def fwd_chunk(j):
        return j

    def bwd_chunk(j):
        return jnp.where(j < nctx, nctx - 1 - j, nch - 1 + nctx - j)

    def streams(chunk):
        return [
            pl.BlockSpec((batch, lc, SSD_D_INNER), lambda j: (0, chunk(j), 0)),
            pl.BlockSpec((batch, lc, SSD_BC_WIDTH), lambda j: (0, chunk(j), 0)),
            pl.BlockSpec((batch, lc, DT_WIDTH), lambda j: (0, chunk(j), 0)),
            pl.BlockSpec((batch, DT_WIDTH, lc), lambda j: (0, 0, chunk(j))),
        ]

    by_batch = lambda a: a.reshape(batch, t_per_batch, a.shape[-1])
    data = (by_batch(xs), by_batch(bc), by_batch(dt_raw), dtT_raw)
    expand = (jnp.arange(SSD_D_INNER)[None, :] // SSD_HEAD_DIM == jnp.arange(SSD_HEADS)[:, None]).astype(bf16)
    const2 = lambda j: (0, 0)
    y_f, y_b = pl.pallas_call(
        _ssd_kernel,
        grid=(nch,),
        in_specs=streams(fwd_chunk) + streams(bwd_chunk) + [
            pl.BlockSpec((2, SSD_HEADS), const2),
            pl.BlockSpec((2, SSD_HEADS, 1), lambda j: (0, 0, 0)),
            pl.BlockSpec((1, DT_WIDTH), const2),
            pl.BlockSpec((DT_WIDTH, 1), const2),
            pl.BlockSpec((2, SSD_HEADS), const2),
            pl.BlockSpec((SSD_HEADS, SSD_D_INNER), const2),
        ],
        out_specs=[
            pl.BlockSpec((batch, lc, SSD_D_INNER), lambda j: (0, fwd_chunk(j), 0)),
            pl.BlockSpec((batch, lc, SSD_D_INNER), lambda j: (0, bwd_chunk(j), 0)),
        ],
        out_shape=[jax.ShapeDtypeStruct((batch, t_per_batch, SSD_D_INNER), bf16)] * 2,
        scratch_shapes=[pltpu.VMEM((batch, 2, SSD_GROUPS, SSD_D_STATE, SSD_GROUP_WIDTH), f32)],
        compiler_params=_cparams(("arbitrary",)),
        name="ssd_scan",
    )(*data, *data,
      a_log.astype(f32), a_log.astype(f32).reshape(2, SSD_HEADS, 1),
      dt_bias.astype(f32).reshape(1, DT_WIDTH), dt_bias.astype(f32).reshape(DT_WIDTH, 1), d_skip.astype(f32), expand)
    return y_f.reshape(m, SSD_D_INNER), y_b.reshape(m, SSD_D_INNER)


def _outproj_kernel(oa_ref, ob_ref, yf_ref, yb_ref, z_ref, sgain_ref, w_ref, x_ref, mod_ref, ngain_ref, wrT_ref, rb_ref,
                    xmid_ref, hf_ref, e_ref, gate_ref, rank_ref, cnt_ref, carry_ref):
    d = D_MODEL
    u = (yf_ref[...].astype(f32) + yb_ref[...].astype(f32)) * _silu(z_ref[...].astype(f32))
    parts = []
    for g in range(SSD_GROUPS):
        ug = u[:, g * SSD_GROUP_WIDTH:(g + 1) * SSD_GROUP_WIDTH]
        parts.append(ug * lax.rsqrt(jnp.mean(ug * ug, axis=-1, keepdims=True) + EPS))
    gn = (jnp.concatenate(parts, axis=-1) * sgain_ref[...]).astype(bf16)
    acc = (_dot(oa_ref[...], w_ref[0:NA_WIDTH, :]) + _dot(ob_ref[...], w_ref[NA_WIDTH:NA_WIDTH + WA_WIDTH, :])
           + _dot(gn, w_ref[NA_WIDTH + WA_WIDTH:, :]))
    mod = mod_ref[0]
    xm = x_ref[...] + mod[:, 2 * d:3 * d] * acc
    xmid_ref[...] = xm
    hf = _rms_mod(xm, ngain_ref[...], mod[:, 3 * d:4 * d], mod[:, 4 * d:5 * d])
    hf_ref[...] = hf
    _route(hf, wrT_ref, rb_ref, e_ref, gate_ref, rank_ref, cnt_ref, carry_ref)


def _out_projection(o_a, o_b, y_f, y_b, z, ssd_norm, w_out, xa, mod4, ngain, w_router, router_bias, t_per_batch):
    m, d = xa.shape
    tm = ROW_BLOCK
    nblk = t_per_batch // tm
    row = lambda i: (i, 0)
    const = lambda i: (0, 0)
    mod_row = lambda i: (jnp.where(i % nblk == 0, 2, i // nblk), 0, 0)
    return pl.pallas_call(
        _outproj_kernel,
        grid=(m // tm,),
        in_specs=[
            pl.BlockSpec((tm, NA_WIDTH), row),
            pl.BlockSpec((tm, WA_WIDTH), row),
            pl.BlockSpec((tm, SSD_D_INNER), row),
            pl.BlockSpec((tm, SSD_D_INNER), row),
            pl.BlockSpec((tm, SSD_D_INNER), row),
            _resident((1, SSD_D_INNER), const),
            _resident((d, d), const),
            pl.BlockSpec((tm, d), row),
            pl.BlockSpec((1, 1, 6 * d), mod_row),
            _resident((1, d), const),
            _resident((N_EXPERTS, d), const),
            _resident((N_EXPERTS, 1), const),
        ],
        out_specs=[
            pl.BlockSpec((tm, d), row),
            pl.BlockSpec((tm, d), row),
            pl.BlockSpec((2, tm), lambda i: (0, i)),
            pl.BlockSpec((2, tm), lambda i: (0, i)),
            pl.BlockSpec((2, tm), lambda i: (0, i)),
            pl.BlockSpec((N_EXPERTS, 128), const),
        ],
        out_shape=[
            jax.ShapeDtypeStruct((m, d), f32),
            jax.ShapeDtypeStruct((m, d), f32),
            jax.ShapeDtypeStruct((2, m), i32),
            jax.ShapeDtypeStruct((2, m), f32),
            jax.ShapeDtypeStruct((2, m), i32),
            jax.ShapeDtypeStruct((N_EXPERTS, 128), i32),
        ],
        scratch_shapes=[pltpu.VMEM((N_EXPERTS, 1), f32)],
        compiler_params=_cparams(("arbitrary",)),
        name="out_projection",
    )(o_a, o_b, y_f, y_b, z, ssd_norm.reshape(1, -1), w_out.astype(bf16), xa, mod4, ngain.reshape(1, d),
      w_router.T.astype(bf16), router_bias.astype(f32).reshape(N_EXPERTS, 1))


def _route(hf, wrT_ref, rb_ref, e_ref, gate_ref, rank_ref, cnt_ref, carry_ref):
    i = pl.program_id(0)
    tm = hf.shape[0]
    ng, ge = N_EXPERT_GROUPS, EXPERTS_PER_GROUP

    @pl.when(i == 0)
    def _():
        carry_ref[...] = jnp.zeros_like(carry_ref)

    aff = _sigmoid(_dot_nt(wrT_ref[...], hf.astype(bf16)))
    sel3 = (aff + rb_ref[...]).reshape(ng, ge, tm)
    aff3 = aff.reshape(ng, ge, tm)
    io = lax.broadcasted_iota(i32, (ng, ge, tm), 1)
    m1 = jnp.max(sel3, axis=1, keepdims=True)
    i1 = jnp.min(jnp.where(sel3 == m1, io, ge), axis=1, keepdims=True)
    rest = jnp.where(io == i1, -jnp.inf, sel3)
    m2 = jnp.max(rest, axis=1, keepdims=True)
    i2 = jnp.min(jnp.where(rest == m2, io, ge), axis=1, keepdims=True)
    a1 = jnp.sum(jnp.where(io == i1, aff3, 0.0), axis=1)
    a2 = jnp.sum(jnp.where(io == i2, aff3, 0.0), axis=1)
    score = (m1 + m2)[:, 0, :]
    gi = lax.broadcasted_iota(i32, (ng, tm), 0)
    best = jnp.max(score, axis=0, keepdims=True)
    gb = jnp.min(jnp.where(score == best, gi, ng), axis=0, keepdims=True)
    picked = gi == gb
    pick_i = lambda a: jnp.sum(jnp.where(picked, a, 0), axis=0, keepdims=True)
    pick_f = lambda a: jnp.sum(jnp.where(picked, a, 0.0), axis=0, keepdims=True)
    e1 = gb * ge + pick_i(i1[:, 0, :])
    e2 = gb * ge + pick_i(i2[:, 0, :])
    g1 = pick_f(a1)
    g2 = pick_f(a2)
    den = g1 + g2
    e_ref[...] = jnp.concatenate([e1, e2], axis=0)
    gate_ref[...] = jnp.concatenate([g1 / den, g2 / den], axis=0)

    ei = lax.broadcasted_iota(i32, (N_EXPERTS, tm), 0)
    o1 = jnp.where(ei == e1, 1.0, 0.0)
    o2 = jnp.where(ei == e2, 1.0, 0.0)
    both = o1 + o2
    upper = jnp.where(lax.broadcasted_iota(i32, (tm, tm), 0) <= lax.broadcasted_iota(i32, (tm, tm), 1), 1.0, 0.0)
    incl = _dot(both.astype(bf16), upper.astype(bf16))
    before = incl - both + carry_ref[...]
    r1 = jnp.sum(o1 * before, axis=0, keepdims=True)
    r2 = jnp.sum(o2 * before, axis=0, keepdims=True)
    rank_ref[...] = jnp.concatenate([r1, r2], axis=0).astype(i32)
    total = carry_ref[...] + jnp.sum(both, axis=1, keepdims=True)
    carry_ref[...] = total
    cnt_ref[...] = jnp.broadcast_to(total, cnt_ref.shape).astype(i32)


def _dispatch_kernel(last_block_ref, tail_ref, slot_ref, hf_ref, xbuf_ref, zero_ref, sem, zero_sem):
    tm = slot_ref.shape[1]

    @pl.when(pl.program_id(0) == 0)
    def _():
        zero_ref[...] = jnp.zeros_like(zero_ref)

        def block_copy(b):
            r0 = pl.multiple_of(b * MOE_BLOCK, MOE_BLOCK)
            return pltpu.make_async_copy(zero_ref, xbuf_ref.at[pl.ds(r0, MOE_BLOCK)], zero_sem)

        def per_expert(fn):
            def body(e, carry):
                @pl.when(last_block_ref[e] >= 0)
                def _():
                    fn(last_block_ref[e])
                return carry
            lax.fori_loop(0, N_EXPERTS, body, 0)

        def per_tail(fn):
            def body(b, carry):
                fn(b)
                return carry
            lax.fori_loop(tail_ref[0], tail_ref[1], body, 0)

        per_expert(lambda b: block_copy(b).start())
        per_tail(lambda b: block_copy(b).start())
        per_expert(lambda b: block_copy(0).wait())
        per_tail(lambda b: block_copy(0).wait())

    def row_copy(r, s):
        return pltpu.make_async_copy(hf_ref.at[pl.ds(r, 1)], xbuf_ref.at[pl.ds(s, 1)], sem)

    def issue(q, carry):
        for u in range(ROW_UNROLL):
            r = q * ROW_UNROLL + u
            for k in range(2):
                row_copy(r, slot_ref[k, r]).start()
        return carry

    lax.fori_loop(0, tm // ROW_UNROLL, issue, 0)

    def drain(q, carry):
        for _ in range(2 * ROW_UNROLL):
            row_copy(0, 0).wait()
        return carry

    lax.fori_loop(0, tm // ROW_UNROLL, drain, 0)


def _dispatch(hf, slot, last_block, tail, n_pad):
    m, d = hf.shape
    tm = ROW_BLOCK
    return pl.pallas_call(
        _dispatch_kernel,
        grid_spec=pltpu.PrefetchScalarGridSpec(
            num_scalar_prefetch=2,
            grid=(m // tm,),
            in_specs=[
                pl.BlockSpec((2, tm), lambda i, lb, tl: (0, i), memory_space=pltpu.SMEM),
                pl.BlockSpec((tm, d), lambda i, lb, tl: (i, 0)),
            ],
            out_specs=pl.BlockSpec(memory_space=pl.ANY),
            scratch_shapes=[pltpu.VMEM((MOE_BLOCK, d), f32), pltpu.SemaphoreType.DMA, pltpu.SemaphoreType.DMA],
        ),
        out_shape=jax.ShapeDtypeStruct((n_pad, d), f32),
        compiler_params=_cparams(("arbitrary",)),
        name="moe_dispatch",
    )(last_block, tail, slot, hf)


def _cast_and_dot(x, w_ref, wb_ref):
    k = wb_ref.shape[0]
    ck = MOE_CAST_ROWS
    acc = None
    for r0 in range(0, k, ck):
        wc = w_ref[r0:r0 + ck, :].astype(bf16)
        wb_ref[r0:r0 + ck, :] = wc
        part = _dot(x[:, r0:r0 + ck], wc)
        acc = part if acc is None else acc + part
    return acc


def _moe_kernel(be_ref, nb_ref, next_ref, x_ref, wg_hbm, wu_hbm, wd_hbm, y_ref,
                wg_st, wu_st, wd_st, wgb_ref, wub_ref, wdb_ref, sems, *, layer):
    i = pl.program_id(0)
    used = i < nb_ref[0]
    fresh = jnp.logical_and(used, jnp.logical_or(i == 0, be_ref[i] != be_ref[jnp.maximum(i - 1, 0)]))
    staged = ((wg_hbm, wg_st), (wu_hbm, wu_st), (wd_hbm, wd_st))

    def stage(j, expert):
        src, dst = staged[j]
        return pltpu.make_async_copy(src.at[layer, expert], dst, sems.at[j])

    @pl.when(i == 0)
    def _():
        for j in range(len(staged)):
            stage(j, be_ref[0]).start()

    @pl.when(fresh)
    def _():
        x = x_ref[...].astype(bf16)
        stage(0, 0).wait()
        g = _cast_and_dot(x, wg_st, wgb_ref)
        stage(1, 0).wait()
        u = _cast_and_dot(x, wu_st, wub_ref)
        h = (_silu(g) * u).astype(bf16)
        stage(2, 0).wait()
        y_ref[...] = _cast_and_dot(h, wd_st, wdb_ref)

        @pl.when(next_ref[i] >= 0)
        def _():
            for j in range(len(staged)):
                stage(j, next_ref[i]).start()

    @pl.when(jnp.logical_and(used, jnp.logical_not(fresh)))
    def _():
        x = x_ref[...].astype(bf16)
        h = (_silu(_dot(x, wgb_ref[...])) * _dot(x, wub_ref[...])).astype(bf16)
        y_ref[...] = _dot(h, wdb_ref[...])

    @pl.when(jnp.logical_not(used))
    def _():
        y_ref[...] = jnp.zeros_like(y_ref)


def _moe_experts(xbuf, block_expert, n_blocks_used, next_expert, wg, wu, wd, layer):
    n_pad, d = xbuf.shape
    tme = MOE_BLOCK
    f = wg.shape[-1]
    hbm = pl.BlockSpec(memory_space=pl.ANY)
    return pl.pallas_call(
        functools.partial(_moe_kernel, layer=layer),
        grid_spec=pltpu.PrefetchScalarGridSpec(
            num_scalar_prefetch=3,
            grid=(n_pad // tme,),
            in_specs=[
                pl.BlockSpec((tme, d), lambda i, be, nb, nx: (jnp.minimum(i, nb[0] - 1), 0)),
                hbm, hbm, hbm,
            ],
            out_specs=pl.BlockSpec((tme, d), lambda i, be, nb, nx: (i, 0)),
            scratch_shapes=[
                pltpu.VMEM((d, f), f32), pltpu.VMEM((d, f), f32), pltpu.VMEM((f, d), f32),
                pltpu.VMEM((d, f), bf16), pltpu.VMEM((d, f), bf16), pltpu.VMEM((f, d), bf16),
                pltpu.SemaphoreType.DMA((3,)),
            ],
        ),
        out_shape=jax.ShapeDtypeStruct((n_pad, d), f32),
        compiler_params=_cparams(("arbitrary",)),
        name="moe_experts",
    )(block_expert, n_blocks_used, next_expert, xbuf, wg, wu, wd)


def _combine_kernel(slot_ref, slot_next_ref, ybuf_ref, gate_ref, x_ref, mod_ref, fgain_ref, o_ref, buf_ref, sems, *,
                    final_norm, n_steps):
    s = pl.program_id(0)
    tm = x_ref.shape[0]
    d = D_MODEL
    cur = s % 2

    def row_copy(b, k, r, src):
        return pltpu.make_async_copy(ybuf_ref.at[pl.ds(src, 1)], buf_ref.at[b, k, pl.ds(r, 1)], sems.at[b])

    def gather(rows_ref, b):
        def issue(q, carry):
            for u in range(ROW_UNROLL):
                r = q * ROW_UNROLL + u
                for k in range(2):
                    row_copy(b, k, r, rows_ref[k, r]).start()
            return carry

        lax.fori_loop(0, tm // ROW_UNROLL, issue, 0)

    @pl.when(s == 0)
    def _():
        gather(slot_ref, 0)

    @pl.when(s + 1 < n_steps)
    def _():
        gather(slot_next_ref, 1 - cur)

    def drain(q, carry):
        for _ in range(ROW_UNROLL):
            for k in range(2):
                row_copy(cur, k, 0, 0).wait()
        return carry

    lax.fori_loop(0, tm // ROW_UNROLL, drain, 0)
    gate = gate_ref[...]
    y = buf_ref[cur, 0] * gate[:, 0:1] + buf_ref[cur, 1] * gate[:, 1:2]
    xn = x_ref[...] + mod_ref[0][:, 5 * d:6 * d] * y
    if final_norm:
        xn = (xn * lax.rsqrt(jnp.mean(xn * xn, axis=-1, keepdims=True) + EPS)) * fgain_ref[...]
    o_ref[...] = xn


def _combine(ybuf, slot, gate_t, xmid, mod4, fgain, batch, t_per_batch, latent_only):
    m, d = xmid.shape
    tm = ROW_BLOCK
    nblk = t_per_batch // tm
    if latent_only:
        per_batch = nblk - 1
        src = lambda s: (s // per_batch) * nblk + 1 + s % per_batch
        modr = lambda s: s // per_batch
    else:
        per_batch = nblk
        src = lambda s: s
        modr = lambda s: jnp.where(s % nblk == 0, 2, s // nblk)
    n_steps = batch * per_batch
    nxt = lambda s: src(jnp.minimum(s + 1, n_steps - 1))
    return pl.pallas_call(
        functools.partial(_combine_kernel, final_norm=latent_only, n_steps=n_steps),
        grid=(n_steps,),
        in_specs=[
            pl.BlockSpec((2, tm), lambda s: (0, src(s)), memory_space=pltpu.SMEM),
            pl.BlockSpec((2, tm), lambda s: (0, nxt(s)), memory_space=pltpu.SMEM),
            pl.BlockSpec(memory_space=pl.ANY),
            pl.BlockSpec((tm, 2), lambda s: (src(s), 0)),
            pl.BlockSpec((tm, d), lambda s: (src(s), 0)),
            pl.BlockSpec((1, 1, 6 * d), lambda s: (modr(s), 0, 0)),
            pl.BlockSpec((1, d), lambda s: (0, 0)),
        ],
        out_specs=pl.BlockSpec((tm, d), lambda s: (s, 0)),
        out_shape=jax.ShapeDtypeStruct((n_steps * tm, d), f32),
        scratch_shapes=[pltpu.VMEM((2, 2, tm, d), f32), pltpu.SemaphoreType.DMA((2,))],
        compiler_params=_cparams(("arbitrary",)),
        name="moe_combine",
    )(slot, slot, ybuf, gate_t, xmid, mod4, fgain.reshape(1, d))


def _moe_plan(expert, rank, counts, n_assign):
    tme = MOE_BLOCK
    n_blocks = -(-n_assign // tme) + N_EXPERTS
    blocks_per_expert = (counts + tme - 1) // tme
    block_end = jnp.cumsum(blocks_per_expert)
    row_start = (block_end - blocks_per_expert) * tme
    chosen = expert[..., None] == jnp.arange(N_EXPERTS)
    slot = rank + jnp.sum(jnp.where(chosen, row_start, 0), axis=-1)
    block_expert = jnp.minimum(jnp.sum(block_end[None, :] <= jnp.arange(n_blocks)[:, None], axis=1), N_EXPERTS - 1)
    last_block = jnp.where(blocks_per_expert > 0, block_end - 1, -1)
    tail = jnp.stack([block_end[-1], jnp.asarray(n_blocks, block_end.dtype)])
    ids = jnp.arange(N_EXPERTS)
    later = (ids[None, :] > ids[:, None]) & (blocks_per_expert > 0)[None, :]
    following = jnp.min(jnp.where(later, ids[None, :], N_EXPERTS), axis=1)
    following = jnp.where(following < N_EXPERTS, following, -1)
    onehot = block_expert[:, None] == ids[None, :]
    next_expert = jnp.sum(jnp.where(onehot, following[None, :], 0), axis=1)
    return (slot.astype(i32), block_expert.astype(i32), block_end[-1:].astype(i32), last_block.astype(i32),
            tail.astype(i32), next_expert.astype(i32), n_blocks * tme)


def kernel(x, c, ctx, c_ctx, w_ada, b_ada, norm_mix, norm_ffn, norm_final, w_in, w_out, na_rpb, wa_sink,
           ssd_conv_w, ssd_conv_b, ssd_a_log, ssd_dt_bias, ssd_d, ssd_norm, w_router, router_bias,
           w_gate, w_up, w_down):
    batch, seq, d = x.shape
    depth = w_ada.shape[0]
    assert d == D_MODEL and ctx.shape[1] == CTX_LEN and seq % ROW_BLOCK == 0 and batch <= 2
    t_per_batch = CTX_LEN + seq
    m = batch * t_per_batch
    rows = seq // GRID_W

    xa = jnp.concatenate([ctx, x], axis=1).reshape(m, d)
    cc = jnp.zeros((8, d), f32).at[0:batch].set(c).at[2].set(c_ctx)
    mod_all = _ada_modulation(cc, w_ada, b_ada).reshape(depth, 8, 1, 6 * d)
    rope_tabs = _rope_tables(seq)

    out = None
    for l in range(depth):
        mod4 = mod_all[l]
        qkv, z, xbc, dt_raw, dtT_raw = _in_projection(xa, mod4, norm_mix[l], w_in[l], rope_tabs, t_per_batch)
        o_a, o_b = _attention(qkv, _na_bias_tables(na_rpb[l], rows), wa_sink[l], batch, t_per_batch)
        xs, bc = _ssd_conv(xbc, ssd_conv_w[l], ssd_conv_b[l], t_per_batch)
        y_f, y_b = _ssd_scan(xs, bc, dt_raw, dtT_raw, ssd_a_log[l], ssd_dt_bias[l], ssd_d[l], batch, t_per_batch)
        xmid, hf, expert, gate, rank, cnt = _out_projection(o_a, o_b, y_f, y_b, z, ssd_norm[l], w_out[l], xa, mod4,
                                                            norm_ffn[l], w_router, router_bias, t_per_batch)
        slot, block_expert, n_used, last_block, tail, next_expert, n_pad = _moe_plan(expert, rank, cnt[:, 0], 2 * m)
        xbuf = _dispatch(hf, slot, last_block, tail, n_pad)
        ybuf = _moe_experts(xbuf, block_expert, n_used, next_expert, w_gate, w_up, w_down, l)
        last = l == depth - 1
        res = _combine(ybuf, slot, gate.T, xmid, mod4, norm_final, batch, t_per_batch, latent_only=last)
        if last:
            out = res.reshape(batch, seq, d)
        else:
            xa = res
    return out
```

```python
import functools

import jax
import jax.numpy as jnp
import numpy as np
from jax import lax
from jax.experimental import pallas as pl
from jax.experimental.pallas import tpu as pltpu

f32 = jnp.float32
bf16 = jnp.bfloat16
i32 = jnp.int32

D_MODEL = 2048
GRID_W = 64
CTX_LEN = 256
EPS = 1e-6
MASK_VALUE = -1e30
HEAD_DIM = 128
NA_WIDTH = D_MODEL // 4
NA_HEADS = NA_WIDTH // HEAD_DIM
NA_WIN_R = 8
NA_WIN_C = 16
WA_WIDTH = D_MODEL // 4
WA_HEADS = WA_WIDTH // HEAD_DIM
WA_KV_HEADS = 2
WA_KV_WIDTH = WA_KV_HEADS * HEAD_DIM
WA_WINDOW = 128
WA_BLOCK = 128
ROPE_BASE = 10000.0
SSD_D_INNER = D_MODEL // 2
SSD_HEAD_DIM = 64
SSD_HEADS = SSD_D_INNER // SSD_HEAD_DIM
SSD_GROUPS = 2
SSD_D_STATE = 128
SSD_CONV = 5
SSD_BC_WIDTH = 2 * SSD_GROUPS * SSD_D_STATE
SSD_XBC_WIDTH = SSD_D_INNER + SSD_BC_WIDTH
SSD_GROUP_WIDTH = SSD_D_INNER // SSD_GROUPS
SSD_HEADS_PER_GROUP = SSD_HEADS // SSD_GROUPS
N_EXPERTS = 32
N_EXPERT_GROUPS = 4
EXPERTS_PER_GROUP = N_EXPERTS // N_EXPERT_GROUPS
D_FF_EXPERT = D_MODEL // 2

QKV_WIDTH = 3 * NA_WIDTH + WA_WIDTH + 2 * WA_KV_WIDTH
ROPE_LO = 3 * NA_WIDTH
ROPE_HI = ROPE_LO + WA_WIDTH + WA_KV_WIDTH
Z_LO = QKV_WIDTH
XBC_LO = Z_LO + SSD_D_INNER
DT_LO = XBC_LO + SSD_XBC_WIDTH
DT_WIDTH = 2 * SSD_HEADS

ROW_BLOCK = 256
SSD_CHUNK = 128
NA_GROUP_ROWS = 4
WA_STEP_BLOCKS = 2
NA_KEY_ROWS = NA_GROUP_ROWS + NA_WIN_R
MOE_BLOCK = 256
MOE_CAST_ROWS = 256
ROW_UNROLL = 8
CONV_HALO = 8
VMEM_LIMIT = 56 * 1024 * 1024


def _cparams(sem):
    return pltpu.CompilerParams(dimension_semantics=sem, vmem_limit_bytes=VMEM_LIMIT)


def _sigmoid(x):
    return 1.0 / (1.0 + jnp.exp(-x))


def _silu(x):
    return x * _sigmoid(x)


def _softplus(x):
    return jnp.maximum(x, 0.0) + jnp.log1p(jnp.exp(-jnp.abs(x)))


def _dot(a, b):
    return jnp.dot(a, b, preferred_element_type=f32)


def _dot_nt(a, b):
    return lax.dot_general(a, b, (((1,), (1,)), ((), ())), preferred_element_type=f32)


def _dot_tn(a, b):
    return lax.dot_general(a, b, (((0,), (0,)), ((), ())), preferred_element_type=f32)


def _resident(shape, index_map):
    return pl.BlockSpec(shape, index_map, pipeline_mode=pl.Buffered(1))


def _ada_kernel(c_ref, w_ref, b_ref, o_ref):
    s = _silu(c_ref[...]).astype(bf16)
    o_ref[0] = _dot(s, w_ref[0].astype(bf16)) + b_ref[0]


def _ada_modulation(cc, w_ada, b_ada):
    depth, d, n = w_ada.shape
    tn = 1024
    return pl.pallas_call(
        _ada_kernel,
        grid=(depth, n // tn),
        in_specs=[
            pl.BlockSpec((8, d), lambda l, j: (0, 0)),
            pl.BlockSpec((1, d, tn), lambda l, j: (l, 0, j)),
            pl.BlockSpec((1, 1, tn), lambda l, j: (l, 0, j)),
        ],
        out_specs=pl.BlockSpec((1, 8, tn), lambda l, j: (l, 0, j)),
        out_shape=jax.ShapeDtypeStruct((depth, 8, n), f32),
        compiler_params=_cparams(("arbitrary", "arbitrary")),
        name="ada_modulation",
    )(cc, w_ada, b_ada.reshape(depth, 1, n))


def _rms_mod(x, gain, shift, scale):
    y = x * lax.rsqrt(jnp.mean(x * x, axis=-1, keepdims=True) + EPS)
    return (y * gain) * (1.0 + scale) + shift


def _inproj_kernel(x_ref, mod_ref, gain_ref, wqkv_ref, wz_ref, wxbc_ref, wdt_ref, wdtT_ref,
                   cos_ref, sina_ref, sinb_ref, qkv_ref, z_ref, xbc_ref, dt_ref, dtT_ref):
    d = D_MODEL
    mod = mod_ref[0]
    h = _rms_mod(x_ref[...], gain_ref[...], mod[:, 0:d], mod[:, d:2 * d]).astype(bf16)
    acc = _dot(h, wqkv_ref[...])
    qkv_ref[:, 0:ROPE_LO] = acc[:, 0:ROPE_LO].astype(bf16)
    cos, sina, sinb = cos_ref[...], sina_ref[...], sinb_ref[...]
    quarter = HEAD_DIM // 4
    for c0 in range(ROPE_LO, ROPE_HI, HEAD_DIM):
        xh = acc[:, c0:c0 + HEAD_DIM]
        rot = (xh * cos + pltpu.roll(xh, HEAD_DIM - quarter, 1) * sina + pltpu.roll(xh, quarter, 1) * sinb)
        qkv_ref[:, c0:c0 + HEAD_DIM] = rot.astype(bf16)
    qkv_ref[:, ROPE_HI:QKV_WIDTH] = acc[:, ROPE_HI:QKV_WIDTH].astype(bf16)
    z_ref[...] = _dot(h, wz_ref[...]).astype(z_ref.dtype)
    xbc_ref[...] = _dot(h, wxbc_ref[...])
    dt_ref[...] = _dot(h, wdt_ref[...])
    dtT_ref[0] = _dot_nt(wdtT_ref[...], h)


def _in_projection(xa, mod4, gain, w_in, rope_tabs, t_per_batch):
    m, d = xa.shape
    tm = ROW_BLOCK
    nblk = t_per_batch // tm
    wb = w_in.astype(bf16)
    wqkv = wb[:, 0:QKV_WIDTH]
    wz = wb[:, Z_LO:XBC_LO]
    wxbc = wb[:, XBC_LO:DT_LO]
    wdt = wb[:, DT_LO:DT_LO + DT_WIDTH]
    wdtT = wdt.T
    cos, sina, sinb = rope_tabs
    row = lambda i: (i, 0)
    const = lambda i: (0, 0)
    mod_row = lambda i: (jnp.where(i % nblk == 0, 2, i // nblk), 0, 0)
    tab_row = lambda i: (i % nblk, 0)
    return pl.pallas_call(
        _inproj_kernel,
        grid=(m // tm,),
        in_specs=[
            pl.BlockSpec((tm, d), row),
            pl.BlockSpec((1, 1, 6 * d), mod_row),
            _resident((1, d), const),
            _resident((d, QKV_WIDTH), const),
            _resident((d, SSD_D_INNER), const),
            _resident((d, SSD_XBC_WIDTH), const),
            _resident((d, DT_WIDTH), const),
            _resident((DT_WIDTH, d), const),
            pl.BlockSpec((tm, HEAD_DIM), tab_row),
            pl.BlockSpec((tm, HEAD_DIM), tab_row),
            pl.BlockSpec((tm, HEAD_DIM), tab_row),
        ],
        out_specs=[
            pl.BlockSpec((tm, QKV_WIDTH), row),
            pl.BlockSpec((tm, SSD_D_INNER), row),
            pl.BlockSpec((tm, SSD_XBC_WIDTH), row),
            pl.BlockSpec((tm, DT_WIDTH), row),
            pl.BlockSpec((1, DT_WIDTH, tm), lambda i: (i // nblk, 0, i % nblk)),
        ],
        out_shape=[
            jax.ShapeDtypeStruct((m, QKV_WIDTH), bf16),
            jax.ShapeDtypeStruct((m, SSD_D_INNER), bf16),
            jax.ShapeDtypeStruct((m, SSD_XBC_WIDTH), f32),
            jax.ShapeDtypeStruct((m, DT_WIDTH), f32),
            jax.ShapeDtypeStruct((m // t_per_batch, DT_WIDTH, t_per_batch), f32),
        ],
        compiler_params=_cparams(("arbitrary",)),
        name="in_projection",
    )(xa, mod4, gain.reshape(1, d), wqkv, wz, wxbc, wdt, wdtT, cos, sina, sinb)


def _rope_tables(seq):
    assert seq % GRID_W == 0
    rows = seq // GRID_W
    q = HEAD_DIM // 4
    inv = jnp.power(ROPE_BASE, -jnp.arange(q, dtype=f32) / q)
    ang_r = jnp.arange(rows, dtype=f32)[:, None] * inv
    ang_c = jnp.arange(GRID_W, dtype=f32)[:, None] * inv
    by_row = lambda a: jnp.repeat(a, GRID_W, axis=0)
    by_col = lambda a: jnp.tile(a, (rows, 1))
    cos_r, sin_r = by_row(jnp.cos(ang_r)), by_row(jnp.sin(ang_r))
    cos_c, sin_c = by_col(jnp.cos(ang_c)), by_col(jnp.sin(ang_c))
    zero = jnp.zeros_like(cos_r)
    cos = jnp.concatenate([cos_r, cos_r, cos_c, cos_c], axis=-1)
    sina = jnp.concatenate([-sin_r, zero, -sin_c, zero], axis=-1)
    sinb = jnp.concatenate([zero, sin_r, zero, sin_c], axis=-1)
    pad = lambda a, v: jnp.concatenate([jnp.full((CTX_LEN, HEAD_DIM), v, f32), a], axis=0)
    return pad(cos, 1.0), pad(sina, 0.0), pad(sinb, 0.0)


def _na_bias_tables(rpb, rows):
    col = np.arange(GRID_W)
    c0 = np.clip(col - NA_WIN_C // 2, 0, GRID_W - NA_WIN_C)
    col_ok = (col[None, :] >= c0[:, None]) & (col[None, :] < c0[:, None] + NA_WIN_C)
    dc = np.clip(col[None, :] - col[:, None] + NA_WIN_C - 1, 0, 2 * NA_WIN_C - 2)
    dc_hot = (dc[:, :, None] == np.arange(2 * NA_WIN_C - 1)).astype(np.float32)
    exact = lax.Precision.HIGHEST
    by_col = jnp.einsum("hde,qke->hdqk", rpb.astype(f32), dc_hot, precision=exact)
    tabs = []
    for r_first in (NA_GROUP_ROWS, 0, rows - NA_GROUP_ROWS):
        start = min(max(r_first - NA_WIN_R // 2, 0), rows - NA_KEY_ROWS)
        r = r_first + np.arange(NA_GROUP_ROWS)
        kr = start + np.arange(NA_KEY_ROWS)
        r0 = np.clip(r - NA_WIN_R // 2, 0, rows - NA_WIN_R)
        row_ok = (kr[None, :] >= r0[:, None]) & (kr[None, :] < r0[:, None] + NA_WIN_R)
        dr = np.clip(kr[None, :] - r[:, None] + NA_WIN_R - 1, 0, 2 * NA_WIN_R - 2)
        dr_hot = (dr[:, :, None] == np.arange(2 * NA_WIN_R - 1)).astype(np.float32)
        b = jnp.einsum("gid,hdqk->hgqik", dr_hot, by_col, precision=exact)
        ok = row_ok[:, None, :, None] & col_ok[None, :, None, :]
        b = jnp.where(ok[None], b, MASK_VALUE)
        tabs.append(b.reshape(rpb.shape[0], NA_GROUP_ROWS * GRID_W, NA_KEY_ROWS * GRID_W))
    return jnp.stack(tabs, axis=1)


def _na_phases(g, q_ref, k_ref, v_ref, bias_ref, o_ref, rows):
    scale = HEAD_DIM ** -0.5
    nk = NA_KEY_ROWS * GRID_W

    def head_cols(h):
        return slice(h * HEAD_DIM, (h + 1) * HEAD_DIM)

    def context():
        for h in range(NA_HEADS):
            hc = head_cols(h)
            s_ctx = _dot_nt(q_ref[:, hc], k_ref[0:CTX_LEN, hc]) * scale
            m = jnp.max(s_ctx, axis=-1, keepdims=True)
            p = jnp.exp(s_ctx - m)
            l = jnp.sum(p, axis=-1, keepdims=True)
            o_ref[:, hc] = (_dot(p.astype(bf16), v_ref[0:CTX_LEN, hc]) / l).astype(bf16)

    def latent():
        r_first = (g - 1) * NA_GROUP_ROWS
        start = jnp.clip(r_first - NA_WIN_R // 2, 0, rows - NA_KEY_ROWS)
        off = pl.multiple_of(CTX_LEN + start * GRID_W, GRID_W)
        for h in range(NA_HEADS):
            hc = head_cols(h)
            q = q_ref[:, hc]
            vc = v_ref[0:CTX_LEN, hc]
            s_ctx = _dot_nt(q, k_ref[0:CTX_LEN, hc]) * scale
            s_loc = _dot_nt(q, k_ref[pl.ds(off, nk), hc]) * scale + bias_ref[h, 0]
            m = jnp.maximum(jnp.max(s_loc, axis=-1, keepdims=True), jnp.max(s_ctx, axis=-1, keepdims=True))
            p_loc = jnp.exp(s_loc - m)
            p_ctx = jnp.exp(s_ctx - m)
            l = jnp.sum(p_loc, axis=-1, keepdims=True) + jnp.sum(p_ctx, axis=-1, keepdims=True)
            o = _dot(p_loc.astype(bf16), v_ref[pl.ds(off, nk), hc]) + _dot(p_ctx.astype(bf16), vc)
            o_ref[:, hc] = (o / l).astype(bf16)

    return context, latent


def _wa_phases(j, sink_ref, q_ref, k_ref, v_ref, o_ref, nb):
    scale = HEAD_DIM ** -0.5
    tb = WA_BLOCK
    g = WA_HEADS // WA_KV_HEADS
    nk = tb + 2 * WA_WINDOW
    ctx_steps = CTX_LEN // (tb * WA_STEP_BLOCKS)
    rowi = lax.broadcasted_iota(i32, (g * tb, 1), 0)

    def stacked_q(sub, kh):
        rows = slice(sub * tb, (sub + 1) * tb)
        return jnp.concatenate([q_ref[rows, (kh * g + i) * HEAD_DIM:(kh * g + i + 1) * HEAD_DIM] for i in range(g)], axis=0)

    def sink_col(kh):
        sink = jnp.full((g * tb, 1), sink_ref[kh * g], f32)
        for i in range(1, g):
            sink = jnp.where(rowi >= i * tb, sink_ref[kh * g + i], sink)
        return sink

    def finish(sub, kh, o):
        for i in range(g):
            o_ref[sub * tb:(sub + 1) * tb, (kh * g + i) * HEAD_DIM:(kh * g + i + 1) * HEAD_DIM] = (
                o[i * tb:(i + 1) * tb].astype(bf16))

    def context():
        for sub in range(WA_STEP_BLOCKS):
            for kh in range(WA_KV_HEADS):
                kcols = slice(kh * HEAD_DIM, (kh + 1) * HEAD_DIM)
                sink = sink_col(kh)
                s_ctx = _dot_nt(stacked_q(sub, kh), k_ref[0:CTX_LEN, kcols]) * scale
                m = jnp.maximum(jnp.max(s_ctx, axis=-1, keepdims=True), sink)
                p = jnp.exp(s_ctx - m)
                l = jnp.sum(p, axis=-1, keepdims=True) + jnp.exp(sink - m)
                finish(sub, kh, _dot(p.astype(bf16), v_ref[0:CTX_LEN, kcols]) / l)

    def latent():
        for sub in range(WA_STEP_BLOCKS):
            n = (j - ctx_steps) * WA_STEP_BLOCKS + sub
            st = jnp.clip(n * tb - WA_WINDOW, 0, nb * tb - nk)
            off = pl.multiple_of(CTX_LEN + st, tb)
            qpos = n * tb + lax.broadcasted_iota(i32, (g * tb, nk), 0) % tb
            kpos = st + lax.broadcasted_iota(i32, (g * tb, nk), 1)
            ok = jnp.abs(qpos - kpos) <= WA_WINDOW
            for kh in range(WA_KV_HEADS):
                kcols = slice(kh * HEAD_DIM, (kh + 1) * HEAD_DIM)
                sink = sink_col(kh)
                q2 = stacked_q(sub, kh)
                s_ctx = _dot_nt(q2, k_ref[0:CTX_LEN, kcols]) * scale
                s_loc = jnp.where(ok, _dot_nt(q2, k_ref[pl.ds(off, nk), kcols]) * scale, MASK_VALUE)
                m = jnp.maximum(jnp.maximum(jnp.max(s_loc, axis=-1, keepdims=True),
                                            jnp.max(s_ctx, axis=-1, keepdims=True)), sink)
                p_loc = jnp.exp(s_loc - m)
                p_ctx = jnp.exp(s_ctx - m)
                l = jnp.sum(p_loc, axis=-1, keepdims=True) + jnp.sum(p_ctx, axis=-1, keepdims=True) + jnp.exp(sink - m)
                o = (_dot(p_loc.astype(bf16), v_ref[pl.ds(off, nk), kcols])
                     + _dot(p_ctx.astype(bf16), v_ref[0:CTX_LEN, kcols]))
                finish(sub, kh, o / l)

    return context, latent


def _attention_kernel(sink_ref, qa_ref, ka_ref, va_ref, bias_ref, qb_ref, kb_ref, vb_ref, oa_ref, ob_ref, *, rows, nb):
    g = pl.program_id(1)
    na_context, na_latent = _na_phases(g, qa_ref, ka_ref, va_ref, bias_ref, oa_ref, rows)
    wa_context, wa_latent = _wa_phases(g, sink_ref, qb_ref, kb_ref, vb_ref, ob_ref, nb)

    @pl.when(g == 0)
    def _():
        na_context()
        wa_context()

    @pl.when(g > 0)
    def _():
        na_latent()
        wa_latent()


def _attention(qkv, bias_tabs, sink, batch, t_per_batch):
    m = qkv.shape[0]
    tq = ROW_BLOCK
    rows = (t_per_batch - CTX_LEN) // GRID_W
    nb = (t_per_batch - CTX_LEN) // WA_BLOCK
    assert tq == CTX_LEN == NA_GROUP_ROWS * GRID_W == WA_STEP_BLOCKS * WA_BLOCK
    assert rows % NA_GROUP_ROWS == 0 and rows >= NA_KEY_ROWS + NA_GROUP_ROWS and nb * WA_BLOCK >= WA_BLOCK + 2 * WA_WINDOW
    ng = rows // NA_GROUP_ROWS
    nblk = t_per_batch // tq
    bias_idx = lambda b, g: (0, jnp.where(g == 1, 1, jnp.where(g == ng, 2, 0)), 0, 0)
    q_spec = lambda width, col: pl.BlockSpec((tq, width), lambda b, g: (b * nblk + g, col))
    kv_spec = lambda width, col: pl.BlockSpec((t_per_batch, width), lambda b, g: (b, col), pipeline_mode=pl.Buffered(1))
    wa_q = ROPE_LO // WA_WIDTH
    wa_k = (ROPE_LO + WA_WIDTH) // WA_KV_WIDTH
    return pl.pallas_call(
        functools.partial(_attention_kernel, rows=rows, nb=nb),
        grid=(batch, nblk),
        in_specs=[
            pl.BlockSpec(memory_space=pltpu.SMEM),
            q_spec(NA_WIDTH, 0),
            kv_spec(NA_WIDTH, 1),
            kv_spec(NA_WIDTH, 2),
            pl.BlockSpec((NA_HEADS, 1, tq, NA_KEY_ROWS * GRID_W), bias_idx),
            q_spec(WA_WIDTH, wa_q),
            kv_spec(WA_KV_WIDTH, wa_k),
            kv_spec(WA_KV_WIDTH, wa_k + 1),
        ],
        out_specs=[q_spec(NA_WIDTH, 0), q_spec(WA_WIDTH, 0)],
        out_shape=[jax.ShapeDtypeStruct((m, NA_WIDTH), bf16), jax.ShapeDtypeStruct((m, WA_WIDTH), bf16)],
        compiler_params=_cparams(("arbitrary", "arbitrary")),
        name="attention",
    )(sink.astype(f32), qkv, qkv, qkv, bias_tabs, qkv, qkv, qkv)


def _conv_kernel(xp_ref, xc_ref, xn_ref, w_ref, b_ref, xs_ref, bc_ref, *, nblk):
    j = pl.program_id(0) % nblk
    tm = xc_ref.shape[0]
    prev_ok = jnp.where(j >= 2, 1.0, 0.0)
    next_ok = jnp.where((j >= 1) & (j <= nblk - 2), 1.0, 0.0)
    ext = jnp.concatenate([xp_ref[...] * prev_ok, xc_ref[...], xn_ref[...] * next_ok], axis=0)
    n_ext = tm + 2 * CONV_HALO
    acc = jnp.zeros((tm, SSD_XBC_WIDTH), f32) + b_ref[...]
    for k in range(SSD_CONV):
        shifted = pltpu.roll(ext, (SSD_CONV // 2 - k) % n_ext, 0)[CONV_HALO:CONV_HALO + tm]
        acc = acc + shifted * w_ref[k:k + 1, :]
    y = _silu(acc)
    xs_ref[...] = y[:, 0:SSD_D_INNER]
    bc_ref[...] = y[:, SSD_D_INNER:].astype(bf16)


def _ssd_conv(xbc, conv_w, conv_b, t_per_batch):
    m = xbc.shape[0]
    tm = ROW_BLOCK
    nblk = t_per_batch // tm
    hb = tm // CONV_HALO
    last = m // CONV_HALO - 1
    return pl.pallas_call(
        functools.partial(_conv_kernel, nblk=nblk),
        grid=(m // tm,),
        in_specs=[
            pl.BlockSpec((CONV_HALO, SSD_XBC_WIDTH), lambda i: (jnp.maximum(i * hb - 1, 0), 0)),
            pl.BlockSpec((tm, SSD_XBC_WIDTH), lambda i: (i, 0)),
            pl.BlockSpec((CONV_HALO, SSD_XBC_WIDTH), lambda i: (jnp.minimum((i + 1) * hb, last), 0)),
            pl.BlockSpec((SSD_CONV, SSD_XBC_WIDTH), lambda i: (0, 0)),
            pl.BlockSpec((1, SSD_XBC_WIDTH), lambda i: (0, 0)),
        ],
        out_specs=[
            pl.BlockSpec((tm, SSD_D_INNER), lambda i: (i, 0)),
            pl.BlockSpec((tm, SSD_BC_WIDTH), lambda i: (i, 0)),
        ],
        out_shape=[
            jax.ShapeDtypeStruct((m, SSD_D_INNER), f32),
            jax.ShapeDtypeStruct((m, SSD_BC_WIDTH), bf16),
        ],
        compiler_params=_cparams(("arbitrary",)),
        name="ssd_conv",
    )(xbc, xbc, xbc, conv_w, conv_b.reshape(1, -1))


def _expand_heads(small, e_ref):
    hi = small.astype(bf16)
    lo = (small - hi.astype(f32)).astype(bf16)
    e = e_ref[...]
    return _dot(hi, e) + _dot(lo, e)


def _ssd_direction(fwd, xs_ref, bc_ref, dt_ref, dtT_ref, alog_ref, alogc_ref, dtb_ref, dtbc_ref, dsk_ref, e_ref,
                   y_ref, state_ref):
    di = 0 if fwd else 1
    lc = SSD_CHUNK
    nh = SSD_HEADS
    gw = SSD_GROUP_WIDTH
    hs = slice(di * nh, (di + 1) * nh)
    dt = _softplus(dt_ref[:, hs] + dtb_ref[:, hs])
    dtT = _softplus(dtT_ref[hs, :] + dtbc_ref[hs, :])
    a_row = -jnp.exp(alog_ref[di:di + 1, :])
    a_col = -jnp.exp(alogc_ref[di])
    d_row = dsk_ref[di:di + 1, :]
    da = dt * a_row
    daT = dtT * a_col
    ri = lax.broadcasted_iota(i32, (lc, lc), 0)
    ci = lax.broadcasted_iota(i32, (lc, lc), 1)
    tri = (ri >= ci) if fwd else (ri <= ci)
    trif = tri.astype(f32)
    acs = jnp.dot(trif, da, preferred_element_type=f32, precision=lax.Precision.HIGHEST)
    acsT = lax.dot_general(daT, trif, (((1,), (1,)), ((), ())), preferred_element_type=f32,
                           precision=lax.Precision.HIGHEST)
    tot = jnp.sum(da, axis=0, keepdims=True)
    small = jnp.concatenate([jnp.exp(acs), jnp.exp(tot - acs) * dt,
                             jnp.broadcast_to(jnp.exp(tot), (8, nh)), jnp.broadcast_to(d_row, (8, nh))], axis=0)
    big = _expand_heads(small, e_ref)
    eacs_x = big[0:lc]
    w_x = big[lc:2 * lc]
    dec_x = big[2 * lc:2 * lc + 1]
    dsk_x = big[2 * lc + 8:2 * lc + 9]

    x = xs_ref[...]
    xw = (x * w_x).astype(bf16)
    xb = x.astype(bf16)
    for g in range(SSD_GROUPS):
        lo = g * gw
        bg = bc_ref[:, g * SSD_D_STATE:(g + 1) * SSD_D_STATE]
        cg = bc_ref[:, (SSD_GROUPS + g) * SSD_D_STATE:(SSD_GROUPS + g + 1) * SSD_D_STATE]
        cb = _dot_nt(cg, bg)
        st = state_ref[di, g]
        y_inter = _dot(cg, st.astype(bf16))
        state_ref[di, g] = st * dec_x[:, lo:lo + gw] + _dot_tn(bg, xw[:, lo:lo + gw])
        ys = []
        for k in range(SSD_HEADS_PER_GROUP):
            h = g * SSD_HEADS_PER_GROUP + k
            seg = jnp.exp(jnp.where(tri, acs[:, h:h + 1] - acsT[h:h + 1, :], -jnp.inf))
            mat = (cb * seg * dtT[h:h + 1, :]).astype(bf16)
            ys.append(_dot(mat, xb[:, h * SSD_HEAD_DIM:(h + 1) * SSD_HEAD_DIM]))
        y_g = jnp.concatenate(ys, axis=-1) + y_inter * eacs_x[:, lo:lo + gw] + x[:, lo:lo + gw] * dsk_x[:, lo:lo + gw]
        y_ref[:, lo:lo + gw] = y_g.astype(y_ref.dtype)


def _ssd_kernel(xsf_ref, bcf_ref, dtf_ref, dtTf_ref, xsb_ref, bcb_ref, dtb_ref, dtTb_ref,
                alog_ref, alogc_ref, bias_ref, biasc_ref, dsk_ref, e_ref, yf_ref, yb_ref, state_ref):
    @pl.when(pl.program_id(0) == 0)
    def _():
        state_ref[...] = jnp.zeros_like(state_ref)

    params = (alog_ref, alogc_ref, bias_ref, biasc_ref, dsk_ref, e_ref)
    for b in range(state_ref.shape[0]):
        _ssd_direction(True, xsf_ref.at[b], bcf_ref.at[b], dtf_ref.at[b], dtTf_ref.at[b], *params, yf_ref.at[b],
                       state_ref.at[b])
        _ssd_direction(False, xsb_ref.at[b], bcb_ref.at[b], dtb_ref.at[b], dtTb_ref.at[b], *params, yb_ref.at[b],
                       state_ref.at[b])


def _ssd_scan(xs, bc, dt_raw, dtT_raw, a_log, dt_bias, d_skip, batch, t_per_batch):
    m = xs.shape[0]
    lc = SSD_CHUNK
    nch = t_per_batch // lc
    nctx = CTX_LEN // lc

    def fwd_chunk(j):
        return j

    def bwd_chunk(j):
        return jnp.where(j < nctx, nctx - 1 - j, nch - 1 + nctx - j)

    def streams(chunk):
        return [
            pl.BlockSpec((batch, lc, SSD_D_INNER), lambda j: (0, chunk(j), 0)),
            pl.BlockSpec((batch, lc, SSD_BC_WIDTH), lambda j: (0, chunk(j), 0)),
            pl.BlockSpec((batch, lc, DT_WIDTH), lambda j: (0, chunk(j), 0)),
            pl.BlockSpec((batch, DT_WIDTH, lc), lambda j: (0, 0, chunk(j))),
        ]

    by_batch = lambda a: a.reshape(batch, t_per_batch, a.shape[-1])
    data = (by_batch(xs), by_batch(bc), by_batch(dt_raw), dtT_raw)
    expand = (jnp.arange(SSD_D_INNER)[None, :] // SSD_HEAD_DIM == jnp.arange(SSD_HEADS)[:, None]).astype(bf16)
    const2 = lambda j: (0, 0)
    y_f, y_b = pl.pallas_call(
        _ssd_kernel,
        grid=(nch,),
        in_specs=streams(fwd_chunk) + streams(bwd_chunk) + [
            pl.BlockSpec((2, SSD_HEADS), const2),
            pl.BlockSpec((2, SSD_HEADS, 1), lambda j: (0, 0, 0)),
            pl.BlockSpec((1, DT_WIDTH), const2),
            pl.BlockSpec((DT_WIDTH, 1), const2),
            pl.BlockSpec((2, SSD_HEADS), const2),
            pl.BlockSpec((SSD_HEADS, SSD_D_INNER), const2),
        ],
        out_specs=[
            pl.BlockSpec((batch, lc, SSD_D_INNER), lambda j: (0, fwd_chunk(j), 0)),
            pl.BlockSpec((batch, lc, SSD_D_INNER), lambda j: (0, bwd_chunk(j), 0)),
        ],
        out_shape=[jax.ShapeDtypeStruct((batch, t_per_batch, SSD_D_INNER), bf16)] * 2,
        scratch_shapes=[pltpu.VMEM((batch, 2, SSD_GROUPS, SSD_D_STATE, SSD_GROUP_WIDTH), f32)],
        compiler_params=_cparams(("arbitrary",)),
        name="ssd_scan",
    )(*data, *data,
      a_log.astype(f32), a_log.astype(f32).reshape(2, SSD_HEADS, 1),
      dt_bias.astype(f32).reshape(1, DT_WIDTH), dt_bias.astype(f32).reshape(DT_WIDTH, 1), d_skip.astype(f32), expand)
    return y_f.reshape(m, SSD_D_INNER), y_b.reshape(m, SSD_D_INNER)


def _outproj_kernel(oa_ref, ob_ref, yf_ref, yb_ref, z_ref, sgain_ref, w_ref, x_ref, mod_ref, ngain_ref, wrT_ref, rb_ref,
                    xmid_ref, hf_ref, e_ref, gate_ref, rank_ref, cnt_ref, carry_ref):
    d = D_MODEL
    u = (yf_ref[...].astype(f32) + yb_ref[...].astype(f32)) * _silu(z_ref[...].astype(f32))
    parts = []
    for g in range(SSD_GROUPS):
        ug = u[:, g * SSD_GROUP_WIDTH:(g + 1) * SSD_GROUP_WIDTH]
        parts.append(ug * lax.rsqrt(jnp.mean(ug * ug, axis=-1, keepdims=True) + EPS))
    gn = (jnp.concatenate(parts, axis=-1) * sgain_ref[...]).astype(bf16)
    acc = (_dot(oa_ref[...], w_ref[0:NA_WIDTH, :]) + _dot(ob_ref[...], w_ref[NA_WIDTH:NA_WIDTH + WA_WIDTH, :])
           + _dot(gn, w_ref[NA_WIDTH + WA_WIDTH:, :]))
    mod = mod_ref[0]
    xm = x_ref[...] + mod[:, 2 * d:3 * d] * acc
    xmid_ref[...] = xm
    hf = _rms_mod(xm, ngain_ref[...], mod[:, 3 * d:4 * d], mod[:, 4 * d:5 * d])
    hf_ref[...] = hf
    _route(hf, wrT_ref, rb_ref, e_ref, gate_ref, rank_ref, cnt_ref, carry_ref)


def _out_projection(o_a, o_b, y_f, y_b, z, ssd_norm, w_out, xa, mod4, ngain, w_router, router_bias, t_per_batch):
    m, d = xa.shape
    tm = ROW_BLOCK
    nblk = t_per_batch // tm
    row = lambda i: (i, 0)
    const = lambda i: (0, 0)
    mod_row = lambda i: (jnp.where(i % nblk == 0, 2, i // nblk), 0, 0)
    return pl.pallas_call(
        _outproj_kernel,
        grid=(m // tm,),
        in_specs=[
            pl.BlockSpec((tm, NA_WIDTH), row),
            pl.BlockSpec((tm, WA_WIDTH), row),
            pl.BlockSpec((tm, SSD_D_INNER), row),
            pl.BlockSpec((tm, SSD_D_INNER), row),
            pl.BlockSpec((tm, SSD_D_INNER), row),
            _resident((1, SSD_D_INNER), const),
            _resident((d, d), const),
            pl.BlockSpec((tm, d), row),
            pl.BlockSpec((1, 1, 6 * d), mod_row),
            _resident((1, d), const),
            _resident((N_EXPERTS, d), const),
            _resident((N_EXPERTS, 1), const),
        ],
        out_specs=[
            pl.BlockSpec((tm, d), row),
            pl.BlockSpec((tm, d), row),
            pl.BlockSpec((2, tm), lambda i: (0, i)),
            pl.BlockSpec((2, tm), lambda i: (0, i)),
            pl.BlockSpec((2, tm), lambda i: (0, i)),
            pl.BlockSpec((N_EXPERTS, 128), const),
        ],
        out_shape=[
            jax.ShapeDtypeStruct((m, d), f32),
            jax.ShapeDtypeStruct((m, d), f32),
            jax.ShapeDtypeStruct((2, m), i32),
            jax.ShapeDtypeStruct((2, m), f32),
            jax.ShapeDtypeStruct((2, m), i32),
            jax.ShapeDtypeStruct((N_EXPERTS, 128), i32),
        ],
        scratch_shapes=[pltpu.VMEM((N_EXPERTS, 1), f32)],
        compiler_params=_cparams(("arbitrary",)),
        name="out_projection",
    )(o_a, o_b, y_f, y_b, z, ssd_norm.reshape(1, -1), w_out.astype(bf16), xa, mod4, ngain.reshape(1, d),
      w_router.T.astype(bf16), router_bias.astype(f32).reshape(N_EXPERTS, 1))


def _route(hf, wrT_ref, rb_ref, e_ref, gate_ref, rank_ref, cnt_ref, carry_ref):
    i = pl.program_id(0)
    tm = hf.shape[0]
    ng, ge = N_EXPERT_GROUPS, EXPERTS_PER_GROUP

    @pl.when(i == 0)
    def _():
        carry_ref[...] = jnp.zeros_like(carry_ref)

    aff = _sigmoid(_dot_nt(wrT_ref[...], hf.astype(bf16)))
    sel3 = (aff + rb_ref[...]).reshape(ng, ge, tm)
    aff3 = aff.reshape(ng, ge, tm)
    io = lax.broadcasted_iota(i32, (ng, ge, tm), 1)
    m1 = jnp.max(sel3, axis=1, keepdims=True)
    i1 = jnp.min(jnp.where(sel3 == m1, io, ge), axis=1, keepdims=True)
    rest = jnp.where(io == i1, -jnp.inf, sel3)
    m2 = jnp.max(rest, axis=1, keepdims=True)
    i2 = jnp.min(jnp.where(rest == m2, io, ge), axis=1, keepdims=True)
    a1 = jnp.sum(jnp.where(io == i1, aff3, 0.0), axis=1)
    a2 = jnp.sum(jnp.where(io == i2, aff3, 0.0), axis=1)
    score = (m1 + m2)[:, 0, :]
    gi = lax.broadcasted_iota(i32, (ng, tm), 0)
    best = jnp.max(score, axis=0, keepdims=True)
    gb = jnp.min(jnp.where(score == best, gi, ng), axis=0, keepdims=True)
    picked = gi == gb
    pick_i = lambda a: jnp.sum(jnp.where(picked, a, 0), axis=0, keepdims=True)
    pick_f = lambda a: jnp.sum(jnp.where(picked, a, 0.0), axis=0, keepdims=True)
    e1 = gb * ge + pick_i(i1[:, 0, :])
    e2 = gb * ge + pick_i(i2[:, 0, :])
    g1 = pick_f(a1)
    g2 = pick_f(a2)
    den = g1 + g2
    e_ref[...] = jnp.concatenate([e1, e2], axis=0)
    gate_ref[...] = jnp.concatenate([g1 / den, g2 / den], axis=0)

    ei = lax.broadcasted_iota(i32, (N_EXPERTS, tm), 0)
    o1 = jnp.where(ei == e1, 1.0, 0.0)
    o2 = jnp.where(ei == e2, 1.0, 0.0)
    both = o1 + o2
    upper = jnp.where(lax.broadcasted_iota(i32, (tm, tm), 0) <= lax.broadcasted_iota(i32, (tm, tm), 1), 1.0, 0.0)
    incl = _dot(both.astype(bf16), upper.astype(bf16))
    before = incl - both + carry_ref[...]
    r1 = jnp.sum(o1 * before, axis=0, keepdims=True)
    r2 = jnp.sum(o2 * before, axis=0, keepdims=True)
    rank_ref[...] = jnp.concatenate([r1, r2], axis=0).astype(i32)
    total = carry_ref[...] + jnp.sum(both, axis=1, keepdims=True)
    carry_ref[...] = total
    cnt_ref[...] = jnp.broadcast_to(total, cnt_ref.shape).astype(i32)


def _dispatch_kernel(last_block_ref, tail_ref, slot_ref, hf_ref, xbuf_ref, zero_ref, sem, zero_sem):
    tm = slot_ref.shape[1]

    @pl.when(pl.program_id(0) == 0)
    def _():
        zero_ref[...] = jnp.zeros_like(zero_ref)

        def block_copy(b):
            r0 = pl.multiple_of(b * MOE_BLOCK, MOE_BLOCK)
            return pltpu.make_async_copy(zero_ref, xbuf_ref.at[pl.ds(r0, MOE_BLOCK)], zero_sem)

        def per_expert(fn):
            def body(e, carry):
                @pl.when(last_block_ref[e] >= 0)
                def _():
                    fn(last_block_ref[e])
                return carry
            lax.fori_loop(0, N_EXPERTS, body, 0)

        def per_tail(fn):
            def body(b, carry):
                fn(b)
                return carry
            lax.fori_loop(tail_ref[0], tail_ref[1], body, 0)

        per_expert(lambda b: block_copy(b).start())
        per_tail(lambda b: block_copy(b).start())
        per_expert(lambda b: block_copy(0).wait())
        per_tail(lambda b: block_copy(0).wait())

    def row_copy(r, s):
        return pltpu.make_async_copy(hf_ref.at[pl.ds(r, 1)], xbuf_ref.at[pl.ds(s, 1)], sem)

    def issue(q, carry):
        for u in range(ROW_UNROLL):
            r = q * ROW_UNROLL + u
            for k in range(2):
                row_copy(r, slot_ref[k, r]).start(priority=k)
        return carry

    lax.fori_loop(0, tm // ROW_UNROLL, issue, 0)

    def drain(q, carry):
        for _ in range(2 * ROW_UNROLL):
            row_copy(0, 0).wait()
        return carry

    lax.fori_loop(0, tm // ROW_UNROLL, drain, 0)


def _dispatch(hf, slot, last_block, tail, n_pad):
    m, d = hf.shape
    tm = ROW_BLOCK
    return pl.pallas_call(
        _dispatch_kernel,
        grid_spec=pltpu.PrefetchScalarGridSpec(
            num_scalar_prefetch=2,
            grid=(m // tm,),
            in_specs=[
                pl.BlockSpec((2, tm), lambda i, lb, tl: (0, i), memory_space=pltpu.SMEM),
                pl.BlockSpec((tm, d), lambda i, lb, tl: (i, 0)),
            ],
            out_specs=pl.BlockSpec(memory_space=pl.ANY),
            scratch_shapes=[pltpu.VMEM((MOE_BLOCK, d), f32), pltpu.SemaphoreType.DMA, pltpu.SemaphoreType.DMA],
        ),
        out_shape=jax.ShapeDtypeStruct((n_pad, d), f32),
        compiler_params=_cparams(("arbitrary",)),
        name="moe_dispatch",
    )(last_block, tail, slot, hf)


def _cast_and_dot(x, w_ref, wb_ref):
    k = wb_ref.shape[0]
    ck = MOE_CAST_ROWS
    acc = None
    for r0 in range(0, k, ck):
        wc = w_ref[r0:r0 + ck, :].astype(bf16)
        wb_ref[r0:r0 + ck, :] = wc
        part = _dot(x[:, r0:r0 + ck], wc)
        acc = part if acc is None else acc + part
    return acc


def _moe_kernel(be_ref, nb_ref, next_ref, x_ref, wg_hbm, wu_hbm, wd_hbm, y_ref,
                wg_st, wu_st, wd_st, wgb_ref, wub_ref, wdb_ref, sems, *, layer):
    i = pl.program_id(0)
    used = i < nb_ref[0]
    fresh = jnp.logical_and(used, jnp.logical_or(i == 0, be_ref[i] != be_ref[jnp.maximum(i - 1, 0)]))
    staged = ((wg_hbm, wg_st), (wu_hbm, wu_st), (wd_hbm, wd_st))

    def stage(j, expert):
        src, dst = staged[j]
        return pltpu.make_async_copy(src.at[layer, expert], dst, sems.at[j])

    @pl.when(i == 0)
    def _():
        for j in range(len(staged)):
            stage(j, be_ref[0]).start()

    @pl.when(fresh)
    def _():
        x = x_ref[...].astype(bf16)
        stage(0, 0).wait()
        g = _cast_and_dot(x, wg_st, wgb_ref)
        stage(1, 0).wait()
        u = _cast_and_dot(x, wu_st, wub_ref)
        h = (_silu(g) * u).astype(bf16)
        stage(2, 0).wait()
        y_ref[...] = _cast_and_dot(h, wd_st, wdb_ref)

        @pl.when(next_ref[i] >= 0)
        def _():
            for j in range(len(staged)):
                stage(j, next_ref[i]).start(priority=1)

    @pl.when(jnp.logical_and(used, jnp.logical_not(fresh)))
    def _():
        x = x_ref[...].astype(bf16)
        h = (_silu(_dot(x, wgb_ref[...])) * _dot(x, wub_ref[...])).astype(bf16)
        y_ref[...] = _dot(h, wdb_ref[...])

    @pl.when(jnp.logical_not(used))
    def _():
        y_ref[...] = jnp.zeros_like(y_ref)


def _moe_experts(xbuf, block_expert, n_blocks_used, next_expert, wg, wu, wd, layer):
    n_pad, d = xbuf.shape
    tme = MOE_BLOCK
    f = wg.shape[-1]
    hbm = pl.BlockSpec(memory_space=pl.ANY)
    return pl.pallas_call(
        functools.partial(_moe_kernel, layer=layer),
        grid_spec=pltpu.PrefetchScalarGridSpec(
            num_scalar_prefetch=3,
            grid=(n_pad // tme,),
            in_specs=[
                pl.BlockSpec((tme, d), lambda i, be, nb, nx: (jnp.minimum(i, nb[0] - 1), 0)),
                hbm, hbm, hbm,
            ],
            out_specs=pl.BlockSpec((tme, d), lambda i, be, nb, nx: (i, 0)),
            scratch_shapes=[
                pltpu.VMEM((d, f), f32), pltpu.VMEM((d, f), f32), pltpu.VMEM((f, d), f32),
                pltpu.VMEM((d, f), bf16), pltpu.VMEM((d, f), bf16), pltpu.VMEM((f, d), bf16),
                pltpu.SemaphoreType.DMA((3,)),
            ],
        ),
        out_shape=jax.ShapeDtypeStruct((n_pad, d), f32),
        compiler_params=_cparams(("arbitrary",)),
        name="moe_experts",
    )(block_expert, n_blocks_used, next_expert, xbuf, wg, wu, wd)


def _combine_kernel(slot_ref, slot_next_ref, ybuf_ref, gate_ref, x_ref, mod_ref, fgain_ref, o_ref, buf_ref, sems, *,
                    final_norm, n_steps):
    s = pl.program_id(0)
    tm = x_ref.shape[0]
    d = D_MODEL
    cur = s % 2

    def row_copy(b, k, r, src):
        return pltpu.make_async_copy(ybuf_ref.at[pl.ds(src, 1)], buf_ref.at[b, k, pl.ds(r, 1)], sems.at[b])

    def gather(rows_ref, b):
        def issue(q, carry):
            for u in range(ROW_UNROLL):
                r = q * ROW_UNROLL + u
                for k in range(2):
                    row_copy(b, k, r, rows_ref[k, r]).start(priority=k)
            return carry

        lax.fori_loop(0, tm // ROW_UNROLL, issue, 0)

    @pl.when(s == 0)
    def _():
        gather(slot_ref, 0)

    @pl.when(s + 1 < n_steps)
    def _():
        gather(slot_next_ref, 1 - cur)

    def drain(q, carry):
        for _ in range(ROW_UNROLL):
            for k in range(2):
                row_copy(cur, k, 0, 0).wait()
        return carry

    lax.fori_loop(0, tm // ROW_UNROLL, drain, 0)
    gate = gate_ref[...]
    y = buf_ref[cur, 0] * gate[:, 0:1] + buf_ref[cur, 1] * gate[:, 1:2]
    xn = x_ref[...] + mod_ref[0][:, 5 * d:6 * d] * y
    if final_norm:
        xn = (xn * lax.rsqrt(jnp.mean(xn * xn, axis=-1, keepdims=True) + EPS)) * fgain_ref[...]
    o_ref[...] = xn


def _combine(ybuf, slot, gate_t, xmid, mod4, fgain, batch, t_per_batch, latent_only):
    m, d = xmid.shape
    tm = ROW_BLOCK
    nblk = t_per_batch // tm
    if latent_only:
        per_batch = nblk - 1
        src = lambda s: (s // per_batch) * nblk + 1 + s % per_batch
        modr = lambda s: s // per_batch
    else:
        per_batch = nblk
        src = lambda s: s
        modr = lambda s: jnp.where(s % nblk == 0, 2, s // nblk)
    n_steps = batch * per_batch
    nxt = lambda s: src(jnp.minimum(s + 1, n_steps - 1))
    return pl.pallas_call(
        functools.partial(_combine_kernel, final_norm=latent_only, n_steps=n_steps),
        grid=(n_steps,),
        in_specs=[
            pl.BlockSpec((2, tm), lambda s: (0, src(s)), memory_space=pltpu.SMEM),
            pl.BlockSpec((2, tm), lambda s: (0, nxt(s)), memory_space=pltpu.SMEM),
            pl.BlockSpec(memory_space=pl.ANY),
            pl.BlockSpec((tm, 2), lambda s: (src(s), 0)),
            pl.BlockSpec((tm, d), lambda s: (src(s), 0)),
            pl.BlockSpec((1, 1, 6 * d), lambda s: (modr(s), 0, 0)),
            pl.BlockSpec((1, d), lambda s: (0, 0)),
        ],
        out_specs=pl.BlockSpec((tm, d), lambda s: (s, 0)),
        out_shape=jax.ShapeDtypeStruct((n_steps * tm, d), f32),
        scratch_shapes=[pltpu.VMEM((2, 2, tm, d), f32), pltpu.SemaphoreType.DMA((2,))],
        compiler_params=_cparams(("arbitrary",)),
        name="moe_combine",
    )(slot, slot, ybuf, gate_t, xmid, mod4, fgain.reshape(1, d))


def _moe_plan(expert, rank, counts, n_assign):
    tme = MOE_BLOCK
    n_blocks = -(-n_assign // tme) + N_EXPERTS
    blocks_per_expert = (counts + tme - 1) // tme
    block_end = jnp.cumsum(blocks_per_expert)
    row_start = (block_end - blocks_per_expert) * tme
    chosen = expert[..., None] == jnp.arange(N_EXPERTS)
    slot = rank + jnp.sum(jnp.where(chosen, row_start, 0), axis=-1)
    block_expert = jnp.minimum(jnp.sum(block_end[None, :] <= jnp.arange(n_blocks)[:, None], axis=1), N_EXPERTS - 1)
    last_block = jnp.where(blocks_per_expert > 0, block_end - 1, -1)
    tail = jnp.stack([block_end[-1], jnp.asarray(n_blocks, block_end.dtype)])
    ids = jnp.arange(N_EXPERTS)
    later = (ids[None, :] > ids[:, None]) & (blocks_per_expert > 0)[None, :]
    following = jnp.min(jnp.where(later, ids[None, :], N_EXPERTS), axis=1)
    following = jnp.where(following < N_EXPERTS, following, -1)
    onehot = block_expert[:, None] == ids[None, :]
    next_expert = jnp.sum(jnp.where(onehot, following[None, :], 0), axis=1)
    return (slot.astype(i32), block_expert.astype(i32), block_end[-1:].astype(i32), last_block.astype(i32),
            tail.astype(i32), next_expert.astype(i32), n_blocks * tme)


def kernel(x, c, ctx, c_ctx, w_ada, b_ada, norm_mix, norm_ffn, norm_final, w_in, w_out, na_rpb, wa_sink,
           ssd_conv_w, ssd_conv_b, ssd_a_log, ssd_dt_bias, ssd_d, ssd_norm, w_router, router_bias,
           w_gate, w_up, w_down):
    batch, seq, d = x.shape
    depth = w_ada.shape[0]
    assert d == D_MODEL and ctx.shape[1] == CTX_LEN and seq % ROW_BLOCK == 0 and batch <= 2
    t_per_batch = CTX_LEN + seq
    m = batch * t_per_batch
    rows = seq // GRID_W

    xa = jnp.concatenate([ctx, x], axis=1).reshape(m, d)
    cc = jnp.zeros((8, d), f32).at[0:batch].set(c).at[2].set(c_ctx)
    mod_all = _ada_modulation(cc, w_ada, b_ada).reshape(depth, 8, 1, 6 * d)
    rope_tabs = _rope_tables(seq)

    out = None
    for l in range(depth):
        mod4 = mod_all[l]
        qkv, z, xbc, dt_raw, dtT_raw = _in_projection(xa, mod4, norm_mix[l], w_in[l], rope_tabs, t_per_batch)
        o_a, o_b = _attention(qkv, _na_bias_tables(na_rpb[l], rows), wa_sink[l], batch, t_per_batch)
        xs, bc = _ssd_conv(xbc, ssd_conv_w[l], ssd_conv_b[l], t_per_batch)
        y_f, y_b = _ssd_scan(xs, bc, dt_raw, dtT_raw, ssd_a_log[l], ssd_dt_bias[l], ssd_d[l], batch, t_per_batch)
        xmid, hf, expert, gate, rank, cnt = _out_projection(o_a, o_b, y_f, y_b, z, ssd_norm[l], w_out[l], xa, mod4,
                                                            norm_ffn[l], w_router, router_bias, t_per_batch)
        slot, block_expert, n_used, last_block, tail, next_expert, n_pad = _moe_plan(expert, rank, cnt[:, 0], 2 * m)
        xbuf = _dispatch(hf, slot, last_block, tail, n_pad)
        ybuf = _moe_experts(xbuf, block_expert, n_used, next_expert, w_gate, w_up, w_down, l)
        last = l == depth - 1
        res = _combine(ybuf, slot, gate.T, xmid, mod4, norm_final, batch, t_per_batch, latent_only=last)
        if last:
            out = res.reshape(batch, seq, d)
        else:
            xa = res
    return out
```
